```python
import jax, jax.numpy as jnp
from jax import lax
import numpy as np

D_MODEL = 2048
BATCH = 4
SEQ = 4096
DEPTH = 1

MIX_WIDTH = D_MODEL
FOURIER_WIDTH = MIX_WIDTH // 2
N_FOURIER_GROUPS = 4
FOURIER_GROUP_DIM = FOURIER_WIDTH // N_FOURIER_GROUPS
ATTN_WIDTH = MIX_WIDTH - FOURIER_WIDTH
HEAD_DIM = 128
N_HEADS = ATTN_WIDTH // HEAD_DIM
WINDOW_DILATIONS = ((128, 1), (512, 4), (2048, 16))
D_FF = 4 * D_MODEL
IN_WIDTH = FOURIER_WIDTH + 3 * ATTN_WIDTH
RMS_EPS = 1e-6

kernel_name = "hybrid_fourier_dilated_alibi_block"


def _rmsnorm(x, g):
    xf = x.astype(jnp.float32)
    inv = lax.rsqrt(jnp.mean(xf * xf, axis=-1, keepdims=True) + RMS_EPS)
    return (xf * inv * g.astype(jnp.float32)).astype(x.dtype)


def _alibi_slopes(n_heads):
    return jnp.asarray(2.0 ** (-8.0 * (np.arange(n_heads) + 1) / n_heads), dtype=jnp.float32)


def _dilated_branch(q, k, v, slopes, dilation, radius):
    B, H, S, Dh = q.shape
    M = S // dilation
    blk = radius
    nb = -(-M // blk)
    Mp = nb * blk

    def to_classes(t):
        return t.reshape(B, H, M, dilation, Dh).transpose(0, 1, 3, 2, 4)

    pad0 = ((0, 0), (0, 0), (0, 0))
    qc = jnp.pad(to_classes(q), pad0 + ((0, Mp - M), (0, 0))).reshape(B, H, dilation, nb, blk, Dh)
    kc = jnp.pad(to_classes(k), pad0 + ((blk, Mp - M + blk), (0, 0))).reshape(B, H, dilation, nb + 2, blk, Dh)
    vc = jnp.pad(to_classes(v), pad0 + ((blk, Mp - M + blk), (0, 0))).reshape(B, H, dilation, nb + 2, blk, Dh)
    kw = jnp.concatenate([kc[:, :, :, :-2], kc[:, :, :, 1:-1], kc[:, :, :, 2:]], axis=4)
    vw = jnp.concatenate([vc[:, :, :, :-2], vc[:, :, :, 1:-1], vc[:, :, :, 2:]], axis=4)

    mq = jnp.arange(nb)[:, None, None] * blk + jnp.arange(blk)[None, :, None]
    mk = jnp.arange(nb)[:, None, None] * blk - blk + jnp.arange(3 * blk)[None, None, :]
    rel = mk - mq
    valid = (jnp.abs(rel) <= radius) & (mk >= 0) & (mk < M)
    dist = (dilation * jnp.abs(rel)).astype(jnp.float32)

    s = jnp.einsum('bhrnqd,bhrnkd->bhrnqk', qc, kw).astype(jnp.float32)
    s = s - slopes[None, :, None, None, None, None] * dist
    s = jnp.where(valid, s, -jnp.inf)
    mx = jnp.max(s, axis=-1, keepdims=True)
    p = jnp.exp(s - mx)
    den = jnp.sum(p, axis=-1)
    o = jnp.einsum('bhrnqk,bhrnkd->bhrnqd', p.astype(v.dtype), vw).astype(jnp.float32) / den[..., None]

    def from_classes(t):
        tail = t.shape[5:]
        t = t.reshape((B, H, dilation, Mp) + tail)[:, :, :, :M]
        t = jnp.moveaxis(t, 2, 3)
        return t.reshape((B, H, S) + tail)

    return from_classes(o), from_classes(mx[..., 0]), from_classes(den)


def _dilated_attention(q, k, v):
    slopes = _alibi_slopes(q.shape[1])
    outs, maxes, dens = [], [], []
    for window, dil in WINDOW_DILATIONS:
        o, m, d = _dilated_branch(q, k, v, slopes, dil, (window // 2) // dil)
        outs.append(o); maxes.append(m); dens.append(d)
    mx = jnp.stack(maxes)
    w = jnp.stack(dens) * jnp.exp(mx - jnp.max(mx, axis=0, keepdims=True))
    o = jnp.sum(w[..., None] * jnp.stack(outs), axis=0) / jnp.sum(w, axis=0)[..., None]
    return o


def _fourier_mix(u, w_f):
    B, S, _ = u.shape
    ug = u.reshape(B, S, N_FOURIER_GROUPS, FOURIER_GROUP_DIM).astype(jnp.float32)
    re = jnp.fft.fft2(ug, axes=(1, 3), norm="ortho").real.astype(u.dtype)
    y = jnp.einsum('bsgc,gce->bsge', re, w_f)
    return y.reshape(B, S, FOURIER_WIDTH)


def setup_inputs(seed: int = 0) -> dict:
    key = jax.random.key(seed)
    ks = jax.random.split(key, 10)
    f32 = jnp.float32
    x = jax.random.normal(ks[0], (BATCH, SEQ, D_MODEL), f32)
    norm_mix_g = 1.0 + 0.02 * jax.random.normal(ks[1], (DEPTH, D_MODEL), f32)
    w_in = jax.random.normal(ks[2], (DEPTH, D_MODEL, IN_WIDTH), f32) * D_MODEL ** -0.5
    w_fourier = jax.random.normal(ks[3], (DEPTH, N_FOURIER_GROUPS, FOURIER_GROUP_DIM, FOURIER_GROUP_DIM), f32) * FOURIER_GROUP_DIM ** -0.5
    w_out = jax.random.normal(ks[4], (DEPTH, MIX_WIDTH, D_MODEL), f32) * MIX_WIDTH ** -0.5
    norm_mlp_g = 1.0 + 0.02 * jax.random.normal(ks[5], (DEPTH, D_MODEL), f32)
    w_up = jax.random.normal(ks[6], (DEPTH, D_MODEL, D_FF), f32) * D_MODEL ** -0.5
    w_down = jax.random.normal(ks[7], (DEPTH, D_FF, D_MODEL), f32) * D_FF ** -0.5
    norm_final_g = 1.0 + 0.02 * jax.random.normal(ks[8], (D_MODEL,), f32)
    return {"x": x, "norm_mix_g": norm_mix_g, "w_in": w_in, "w_fourier": w_fourier,
            "w_out": w_out, "norm_mlp_g": norm_mlp_g, "w_up": w_up, "w_down": w_down,
            "norm_final_g": norm_final_g}


def reference(x, norm_mix_g, w_in, w_fourier, w_out, norm_mlp_g, w_up, w_down, norm_final_g):
    B, S, _ = x.shape
    h = x
    for layer in range(DEPTH):
        u = _rmsnorm(h, norm_mix_g[layer])
        proj = jnp.einsum('bsd,de->bse', u, w_in[layer])
        u_f = proj[..., :FOURIER_WIDTH]
        qkv = proj[..., FOURIER_WIDTH:].reshape(B, S, 3, N_HEADS, HEAD_DIM)
        q = jnp.transpose(qkv[:, :, 0], (0, 2, 1, 3)) * (HEAD_DIM ** -0.5)
        k = jnp.transpose(qkv[:, :, 1], (0, 2, 1, 3))
        v = jnp.transpose(qkv[:, :, 2], (0, 2, 1, 3))
        y_f = _fourier_mix(u_f, w_fourier[layer])
        y_a = _dilated_attention(q, k, v).astype(h.dtype)
        y_a = jnp.transpose(y_a, (0, 2, 1, 3)).reshape(B, S, ATTN_WIDTH)
        y = jnp.concatenate([y_f, y_a], axis=-1)
        h = h + jnp.einsum('bse,ed->bsd', y, w_out[layer])
        u = _rmsnorm(h, norm_mlp_g[layer])
        a = jnp.einsum('bsd,df->bsf', u, w_up[layer])
        a = jnp.square(jax.nn.relu(a))
        h = h + jnp.einsum('bsf,fd->bsd', a, w_down[layer])
    return _rmsnorm(h, norm_final_g)
```

```python
import functools

import numpy as np
import jax
import jax.numpy as jnp
from jax import lax
from jax.experimental import pallas as pl
from jax.experimental.pallas import tpu as pltpu

F32 = jnp.float32
BF16 = jnp.bfloat16

HEAD_DIM = 128
N_FOURIER_GROUPS = 4
WINDOW_DILATIONS = ((128, 1), (512, 4), (2048, 16))
RADIUS = 64
RMS_EPS = 1e-6
MASK_VALUE = -1e30

Q_BLOCK = 128
K_WINDOW = Q_BLOCK + 2 * RADIUS
DFT_RADIX = 4

VMEM_LIMIT_BYTES = 60 * 1024 * 1024


def _params(semantics):
    return pltpu.CompilerParams(dimension_semantics=semantics, vmem_limit_bytes=VMEM_LIMIT_BYTES)


def _resident(block_shape, index_map):
    return pl.BlockSpec(block_shape, index_map, pipeline_mode=pl.Buffered(1))


def _rmsnorm_f32(x, g):
    inv = lax.rsqrt(jnp.mean(x * x, axis=-1, keepdims=True) + RMS_EPS)
    return x * inv * g


def _in_proj_kernel(x_ref, g_ref, w_ref, uf_ref, qkv_ref, u_scr, *, heads_per_block, q_scale):
    j = pl.program_id(1)

    @pl.when(j == 0)
    def _():
        u_scr[...] = _rmsnorm_f32(x_ref[...], g_ref[...]).astype(BF16)
        uf_ref[...] = jnp.dot(u_scr[...], w_ref[...], preferred_element_type=F32).astype(BF16)

    def write_heads(scale):
        acc = jnp.dot(u_scr[...], w_ref[...], preferred_element_type=F32)
        if scale is not None:
            acc = acc * scale
        for hh in range(heads_per_block):
            qkv_ref[0, hh] = acc[:, hh * HEAD_DIM:(hh + 1) * HEAD_DIM].astype(BF16)

    @pl.when(j == 1)
    def _():
        write_heads(q_scale)

    @pl.when(j > 1)
    def _():
        write_heads(None)


def _in_proj(x2, g, w_in, batch, seq, block_m):
    n_tok, d_model = x2.shape
    in_width = w_in.shape[1]
    block_n = in_width // 4
    heads_per_block = block_n // HEAD_DIM
    tiles_per_batch = seq // block_m
    kernel = functools.partial(_in_proj_kernel, heads_per_block=heads_per_block,
                               q_scale=HEAD_DIM ** -0.5)
    return pl.pallas_call(
        kernel,
        grid=(n_tok // block_m, 4),
        in_specs=[
            pl.BlockSpec((block_m, d_model), lambda i, j: (i, 0)),
            _resident((1, d_model), lambda i, j: (0, 0)),
            pl.BlockSpec((d_model, block_n), lambda i, j: (0, j)),
        ],
        out_specs=[
            pl.BlockSpec((block_m, block_n), lambda i, j: (i, 0)),
            pl.BlockSpec((1, heads_per_block, block_m, HEAD_DIM),
                         lambda i, j: (i // tiles_per_batch, jnp.maximum(j - 1, 0),
                                       i % tiles_per_batch, 0)),
        ],
        out_shape=[
            jax.ShapeDtypeStruct((n_tok, block_n), BF16),
            jax.ShapeDtypeStruct((batch, 3 * heads_per_block, seq, HEAD_DIM), BF16),
        ],
        scratch_shapes=[pltpu.VMEM((block_m, d_model), BF16)],
        compiler_params=_params(("arbitrary", "arbitrary")),
        name="in_proj",
    )(x2, g, w_in)


def _dft_tables(seq, group_dim):
    quarter = seq // DFT_RADIX
    s = np.arange(quarter, dtype=np.int64)[:, None]
    tabs = []
    for r in range(DFT_RADIX):
        t = DFT_RADIX * np.arange(quarter, dtype=np.int64)[None, :] + r
        ang = 2.0 * np.pi * ((s * t) % seq).astype(np.float64) / seq
        tabs.append(np.concatenate([np.cos(ang), -np.sin(ang)], axis=0))
    seq_tab = np.stack(tabs).astype(np.float32)
    c = np.arange(group_dim, dtype=np.int64)
    ang = 2.0 * np.pi * ((c[:, None] * c[None, :]) % group_dim).astype(np.float64) / group_dim
    ortho = 1.0 / np.sqrt(float(seq) * float(group_dim))
    chan_tab = (np.concatenate([np.cos(ang), np.sin(ang)], axis=0) * ortho).astype(np.float32)
    return seq_tab, chan_tab


def _fourier_kernel(tab_ref, u0_ref, u1_ref, u2_ref, u3_ref, chan_ref, wf_ref, y_ref):
    quarter = u0_ref.shape[1]
    re, im = [], []
    for r, u_ref in enumerate((u0_ref, u1_ref, u2_ref, u3_ref)):
        g = jnp.dot(tab_ref[r], u_ref[0], preferred_element_type=F32)
        re.append(g[:quarter])
        im.append(g[quarter:])
    z = (
        (re[0] + re[1] + re[2] + re[3], im[0] + im[1] + im[2] + im[3]),
        (re[0] + im[1] - re[2] - im[3], im[0] - re[1] - im[2] + re[3]),
        (re[0] - re[1] + re[2] - re[3], im[0] - im[1] + im[2] - im[3]),
        (re[0] - im[1] - re[2] + im[3], im[0] + re[1] - im[2] - re[3]),
    )
    for q, (zr, zi) in enumerate(z):
        zc = jnp.concatenate([zr, zi], axis=1).astype(BF16)
        real = jnp.dot(zc, chan_ref[...], preferred_element_type=F32).astype(BF16)
        y = jnp.dot(real, wf_ref[0], preferred_element_type=F32)
        y_ref[0, q * quarter:(q + 1) * quarter, :] = y.astype(y_ref.dtype)


def _fourier_mix(uf, w_fourier, batch, seq):
    fw = uf.shape[1]
    n_groups = w_fourier.shape[0]
    group_dim = fw // n_groups
    quarter = seq // DFT_RADIX
    seq_tab, chan_tab = _dft_tables(seq, group_dim)
    seq_tab = jnp.asarray(seq_tab).astype(BF16)
    chan_tab = jnp.asarray(chan_tab).astype(BF16)
    u4 = uf.reshape(batch, quarter, DFT_RADIX * fw)

    def u_spec(r):
        return pl.BlockSpec((1, quarter, group_dim),
                            lambda b, g, r=r: (b, 0, r * n_groups + g))

    return pl.pallas_call(
        _fourier_kernel,
        grid=(batch, n_groups),
        in_specs=[
            _resident((DFT_RADIX, 2 * quarter, quarter), lambda b, g: (0, 0, 0)),
            u_spec(0), u_spec(1), u_spec(2), u_spec(3),
            _resident((2 * group_dim, group_dim), lambda b, g: (0, 0)),
            pl.BlockSpec((1, group_dim, group_dim), lambda b, g: (g, 0, 0)),
        ],
        out_specs=pl.BlockSpec((1, seq, group_dim), lambda b, g: (b, 0, g)),
        out_shape=jax.ShapeDtypeStruct((batch, seq, fw), BF16),
        compiler_params=_params(("arbitrary", "arbitrary")),
        name="fourier_mix",
    )(seq_tab, u4, u4, u4, u4, chan_tab, w_fourier)


def _block_offsets(n_blocks):
    out = []
    for blk in range(n_blocks):
        qs = blk * Q_BLOCK
        ws = min(max(qs - RADIUS, 0), n_blocks * Q_BLOCK - K_WINDOW)
        out.append((qs, ws))
    return out


def _attention_kernel(slopes_ref,
                      q1_ref, k1_ref, v1_ref, q4_ref, k4_ref, v4_ref, q16_ref, k16_ref, v16_ref,
                      o_ref,
                      bias_scr, acc4_scr, m4_scr, l4_scr, acc16_scr, m16_scr, l16_scr):
    seq = q1_ref.shape[0]
    slope = slopes_ref[pl.program_id(1)]

    row = lax.broadcasted_iota(jnp.int32, (Q_BLOCK, K_WINDOW), 0)
    col = lax.broadcasted_iota(jnp.int32, (Q_BLOCK, K_WINDOW), 1)
    offsets = (0, -RADIUS, -2 * RADIUS)
    for gi, (_, dil) in enumerate(WINDOW_DILATIONS):
        for oi, off in enumerate(offsets):
            rel = jnp.abs(col - row + off)
            penalty = slope * (dil * rel).astype(F32)
            bias_scr[gi * 3 + oi] = jnp.where(rel <= RADIUS, -penalty, MASK_VALUE)

    def block(q, k, v, bias):
        s = lax.dot_general(q, k, (((1,), (1,)), ((), ())), preferred_element_type=F32) + bias
        m = jnp.max(s, axis=1, keepdims=True)
        p = jnp.exp(s - m)
        l = jnp.sum(p, axis=1, keepdims=True)
        acc = jnp.dot(p.astype(BF16), v, preferred_element_type=F32)
        return acc, m, l

    def bias_index(gi, qs, ws):
        return gi * 3 + offsets.index(ws - qs)

    def run_class(gi, dil, r, q_ref, k_ref, v_ref, emit):
        members = seq // dil
        lanes = slice(r * HEAD_DIM, (r + 1) * HEAD_DIM)
        blocks = _block_offsets(members // Q_BLOCK)

        def static_block(qs, ws):
            out = block(q_ref[qs:qs + Q_BLOCK, lanes], k_ref[ws:ws + K_WINDOW, lanes],
                        v_ref[ws:ws + K_WINDOW, lanes], bias_scr[bias_index(gi, qs, ws)])
            emit(qs, *out)

        static_block(*blocks[0])
        if len(blocks) > 2:
            mid_bias = gi * 3 + 1

            def body(blk, carry):
                qs = pl.multiple_of(blk * Q_BLOCK, Q_BLOCK)
                ws = pl.multiple_of(qs - RADIUS, RADIUS)
                out = block(q_ref[pl.ds(qs, Q_BLOCK), lanes], k_ref[pl.ds(ws, K_WINDOW), lanes],
                            v_ref[pl.ds(ws, K_WINDOW), lanes], bias_scr[mid_bias])
                emit(qs, *out)
                return carry

            lax.fori_loop(1, len(blocks) - 1, body, 0)
        static_block(*blocks[-1])

    def scatter_to(acc_scr, m_scr, l_scr, dil, r):
        def emit(qs, acc, m, l):
            rows = pl.ds(qs * dil + r, Q_BLOCK, stride=dil)
            acc_scr[rows, :] = acc
            m_scr[rows, :] = m
            l_scr[rows, :] = l
        return emit

    for r in range(16):
        run_class(2, 16, r, q16_ref, k16_ref, v16_ref, scatter_to(acc16_scr, m16_scr, l16_scr, 16, r))
    for r in range(4):
        run_class(1, 4, r, q4_ref, k4_ref, v4_ref, scatter_to(acc4_scr, m4_scr, l4_scr, 4, r))

    def merge(qs, acc1, m1, l1):
        rows = pl.ds(qs, Q_BLOCK)
        m4, m16 = m4_scr[rows, :], m16_scr[rows, :]
        m = jnp.maximum(m1, jnp.maximum(m4, m16))
        a1, a4, a16 = jnp.exp(m1 - m), jnp.exp(m4 - m), jnp.exp(m16 - m)
        den = a1 * l1 + a4 * l4_scr[rows, :] + a16 * l16_scr[rows, :]
        num = a1 * acc1 + a4 * acc4_scr[rows, :] + a16 * acc16_scr[rows, :]
        o_ref[rows, :] = (num / den).astype(o_ref.dtype)

    run_class(0, 1, 0, q1_ref, k1_ref, v1_ref, merge)


def _attention(qkv, batch, seq, n_heads):
    slopes = jnp.asarray(2.0 ** (-8.0 * (np.arange(n_heads) + 1) / n_heads), dtype=F32)
    operands, in_specs = [], []
    for _, dil in WINDOW_DILATIONS:
        view = qkv.reshape(batch, 3 * n_heads, seq // dil, dil * HEAD_DIM)
        for which in range(3):
            operands.append(view)
            in_specs.append(pl.BlockSpec((None, None, seq // dil, dil * HEAD_DIM),
                                         lambda b, h, which=which: (b, which * n_heads + h, 0, 0)))
    column = lambda: pltpu.VMEM((seq, 1), F32)
    full = lambda: pltpu.VMEM((seq, HEAD_DIM), F32)
    return pl.pallas_call(
        _attention_kernel,
        grid=(batch, n_heads),
        in_specs=[pl.BlockSpec(memory_space=pltpu.SMEM)] + in_specs,
        out_specs=pl.BlockSpec((None, seq, HEAD_DIM), lambda b, h: (b, 0, h)),
        out_shape=jax.ShapeDtypeStruct((batch, seq, n_heads * HEAD_DIM), BF16),
        scratch_shapes=[pltpu.VMEM((9, Q_BLOCK, K_WINDOW), F32),
                        full(), column(), column(), full(), column(), column()],
        compiler_params=_params(("arbitrary", "arbitrary")),
        name="dilated_attention",
    )(slopes, *operands)


def _out_proj_kernel(yf_ref, ya_ref, x_ref, w_ref, h_ref):
    fw = yf_ref.shape[1]
    acc = jnp.dot(yf_ref[...], w_ref[:fw, :], preferred_element_type=F32)
    acc = acc + jnp.dot(ya_ref[...], w_ref[fw:, :], preferred_element_type=F32)
    h_ref[...] = x_ref[...] + acc


def _out_proj(yf, ya, x2, w_out, block_m):
    n_tok, d_model = x2.shape
    fw, aw = yf.shape[1], ya.shape[1]
    return pl.pallas_call(
        _out_proj_kernel,
        grid=(n_tok // block_m,),
        in_specs=[
            pl.BlockSpec((block_m, fw), lambda i: (i, 0)),
            pl.BlockSpec((block_m, aw), lambda i: (i, 0)),
            pl.BlockSpec((block_m, d_model), lambda i: (i, 0)),
            _resident((fw + aw, d_model), lambda i: (0, 0)),
        ],
        out_specs=pl.BlockSpec((block_m, d_model), lambda i: (i, 0)),
        out_shape=jax.ShapeDtypeStruct((n_tok, d_model), F32),
        compiler_params=_params(("arbitrary",)),
        name="out_proj",
    )(yf, ya, x2, w_out)


def _mlp_kernel(h_ref, g_ref, wu_ref, wd_ref, gf_ref, o_ref, u_scr, *, row_chunk):
    f = pl.program_id(1)
    last = pl.num_programs(1) - 1

    @pl.when(f == 0)
    def _():
        u_scr[...] = _rmsnorm_f32(h_ref[...], g_ref[...]).astype(BF16)

    def contribution(rows):
        a = jnp.dot(u_scr[rows, :], wu_ref[...], preferred_element_type=F32)
        a = jnp.square(jnp.maximum(a, 0.0)).astype(BF16)
        return jnp.dot(a, wd_ref[...], preferred_element_type=F32)

    chunks = [slice(c, c + row_chunk) for c in range(0, o_ref.shape[0], row_chunk)]

    @pl.when(f == 0)
    def _():
        for rows in chunks:
            o_ref[rows, :] = h_ref[rows, :] + contribution(rows)

    @pl.when(jnp.logical_and(f > 0, f < last))
    def _():
        for rows in chunks:
            o_ref[rows, :] += contribution(rows)

    @pl.when(f == last)
    def _():
        for rows in chunks:
            o_ref[rows, :] = _rmsnorm_f32(o_ref[rows, :] + contribution(rows), gf_ref[...])


def _mlp(h, g_mlp, w_up, w_down, g_final, block_m, block_f, row_chunk):
    n_tok, d_model = h.shape
    d_ff = w_up.shape[1]
    kernel = functools.partial(_mlp_kernel, row_chunk=row_chunk)
    return pl.pallas_call(
        kernel,
        grid=(n_tok // block_m, d_ff // block_f),
        in_specs=[
            pl.BlockSpec((block_m, d_model), lambda i, f: (i, 0)),
            _resident((1, d_model), lambda i, f: (0, 0)),
            pl.BlockSpec((d_model, block_f), lambda i, f: (0, f)),
            pl.BlockSpec((block_f, d_model), lambda i, f: (f, 0)),
            _resident((1, d_model), lambda i, f: (0, 0)),
        ],
        out_specs=pl.BlockSpec((block_m, d_model), lambda i, f: (i, 0)),
        out_shape=jax.ShapeDtypeStruct((n_tok, d_model), F32),
        scratch_shapes=[pltpu.VMEM((block_m, d_model), BF16)],
        compiler_params=_params(("arbitrary", "arbitrary")),
        name="mlp",
    )(h, g_mlp, w_up, w_down, g_final)


def kernel(x, norm_mix_g, w_in, w_fourier, w_out, norm_mlp_g, w_up, w_down, norm_final_g):
    batch, seq, d_model = x.shape
    depth = w_in.shape[0]
    fourier_width = w_fourier.shape[1] * w_fourier.shape[2]
    n_heads = (w_in.shape[2] - fourier_width) // (3 * HEAD_DIM)
    assert w_in.shape[2] == 4 * fourier_width and n_heads * HEAD_DIM == fourier_width
    assert seq % (16 * K_WINDOW) == 0
    assert depth == 1, "the MLP kernel fuses the final RMSNorm, so it handles a single layer"

    h = x.reshape(batch * seq, d_model)
    uf, qkv = _in_proj(h, norm_mix_g[0][None, :], w_in[0].astype(BF16), batch, seq, block_m=1024)
    yf = _fourier_mix(uf, w_fourier[0].astype(BF16), batch, seq)
    ya = _attention(qkv, batch, seq, n_heads)
    h = _out_proj(yf.reshape(batch * seq, fourier_width),
                  ya.reshape(batch * seq, n_heads * HEAD_DIM),
                  h, w_out[0].astype(BF16), block_m=512)
    h = _mlp(h, norm_mlp_g[0][None, :], w_up[0].astype(BF16), w_down[0].astype(BF16),
             norm_final_g[None, :], block_m=1024, block_f=512, row_chunk=512)
    return h.reshape(batch, seq, d_model)
```

```python
import functools

import numpy as np
import jax
import jax.numpy as jnp
from jax import lax
from jax.experimental import pallas as pl
from jax.experimental.pallas import tpu as pltpu

F32 = jnp.float32
BF16 = jnp.bfloat16

HEAD_DIM = 128
WINDOW_DILATIONS = ((128, 1), (512, 4), (2048, 16))
DILATIONS = tuple(d for _, d in WINDOW_DILATIONS)
RADIUS = 64
assert all((w // 2) // d == RADIUS for w, d in WINDOW_DILATIONS)
RMS_EPS = 1e-6
MASK_VALUE = -1e30
LOG2_E = 1.4426950408889634

Q_BLOCK = 128
K_WINDOW = Q_BLOCK + 2 * RADIUS
BLOCKS_PER_STEP = 8
DFT_RADIX = 4

VMEM_LIMIT_BYTES = 60 * 1024 * 1024


def _params(semantics):
    return pltpu.CompilerParams(dimension_semantics=semantics, vmem_limit_bytes=VMEM_LIMIT_BYTES)


def _resident(block_shape, index_map):
    return pl.BlockSpec(block_shape, index_map, pipeline_mode=pl.Buffered(1))


def _rmsnorm_f32(x, g):
    inv = lax.rsqrt(jnp.mean(x * x, axis=-1, keepdims=True) + RMS_EPS)
    return x * inv * g


def _in_proj_kernel(x_ref, g_ref, w_ref, uf_ref, nat_ref, c4_ref, c16_ref, u_scr, acc_scr, *,
                    heads_per_block, q_scale):
    j = pl.program_id(1)
    block_m = u_scr.shape[0]

    @pl.when(j == 0)
    def _():
        u_scr[...] = _rmsnorm_f32(x_ref[...], g_ref[...]).astype(BF16)

    acc = jnp.dot(u_scr[...], w_ref[...], preferred_element_type=F32)
    acc = acc * jnp.where(j == 1, q_scale, 1.0)
    for hh in range(heads_per_block):
        acc_scr[hh] = acc[:, hh * HEAD_DIM:(hh + 1) * HEAD_DIM]

    def residues(hh, dil):
        return [acc_scr[hh, pl.ds(r, block_m // dil, stride=dil), :].astype(BF16)
                for r in range(dil)]

    @pl.when(j == 0)
    def _():
        for hh in range(heads_per_block):
            for r, piece in enumerate(residues(hh, DFT_RADIX)):
                uf_ref[0, r, :, hh * HEAD_DIM:(hh + 1) * HEAD_DIM] = piece

    @pl.when(j > 0)
    def _():
        for hh in range(heads_per_block):
            nat_ref[0, hh, 0] = acc_scr[hh].astype(BF16)
            for r, piece in enumerate(residues(hh, 4)):
                c4_ref[0, hh, r] = piece
            for r, piece in enumerate(residues(hh, 16)):
                c16_ref[0, hh, r] = piece


def _in_proj(x2, g, w_in, batch, seq, block_m):
    n_tok, d_model = x2.shape
    in_width = w_in.shape[1]
    block_n = in_width // 4
    heads_per_block = block_n // HEAD_DIM
    tiles_per_batch = seq // block_m
    kernel = functools.partial(_in_proj_kernel, heads_per_block=heads_per_block,
                               q_scale=HEAD_DIM ** -0.5 * LOG2_E)

    def head_map(i, j):
        return (i // tiles_per_batch, jnp.maximum(j - 1, 0), 0, i % tiles_per_batch, 0)

    def head_block(dil):
        return pl.BlockSpec((1, heads_per_block, dil, block_m // dil, HEAD_DIM), head_map)

    def head_shape(dil):
        return jax.ShapeDtypeStruct((batch, 3 * heads_per_block, dil, seq // dil, HEAD_DIM), BF16)

    return pl.pallas_call(
        kernel,
        grid=(n_tok // block_m, 4),
        in_specs=[
            pl.BlockSpec((block_m, d_model), lambda i, j: (i, 0)),
            _resident((1, d_model), lambda i, j: (0, 0)),
            pl.BlockSpec((d_model, block_n), lambda i, j: (0, j)),
        ],
        out_specs=[
            pl.BlockSpec((1, DFT_RADIX, block_m // DFT_RADIX, block_n),
                         lambda i, j: (i // tiles_per_batch, 0, i % tiles_per_batch, 0)),
            head_block(1), head_block(4), head_block(16),
        ],
        out_shape=[
            jax.ShapeDtypeStruct((batch, DFT_RADIX, seq // DFT_RADIX, block_n), BF16),
            head_shape(1), head_shape(4), head_shape(16),
        ],
        scratch_shapes=[pltpu.VMEM((block_m, d_model), BF16),
                        pltpu.VMEM((heads_per_block, block_m, HEAD_DIM), F32)],
        compiler_params=_params(("arbitrary", "arbitrary")),
        name="in_proj",
    )(x2, g, w_in)


def _dft_tables(seq, group_dim):
    quarter = seq // DFT_RADIX
    s = np.arange(quarter, dtype=np.int64)[:, None]
    tabs = []
    for r in range(DFT_RADIX):
        t = DFT_RADIX * np.arange(quarter, dtype=np.int64)[None, :] + r
        ang = 2.0 * np.pi * ((s * t) % seq).astype(np.float64) / seq
        tabs.append(np.concatenate([np.cos(ang), -np.sin(ang)], axis=0))
    seq_tab = np.stack(tabs).astype(np.float32)
    c = np.arange(group_dim, dtype=np.int64)
    ang = 2.0 * np.pi * ((c[:, None] * c[None, :]) % group_dim).astype(np.float64) / group_dim
    ortho = 1.0 / np.sqrt(float(seq) * float(group_dim))
    chan_tab = (np.concatenate([np.cos(ang), np.sin(ang)], axis=0) * ortho).astype(np.float32)
    return seq_tab, chan_tab


def _fourier_kernel(tab_ref, u0_ref, u1_ref, u2_ref, u3_ref, chan_ref, wf_ref, y_ref):
    quarter = u0_ref.shape[0]
    re, im = [], []
    for r, u_ref in enumerate((u0_ref, u1_ref, u2_ref, u3_ref)):
        g = jnp.dot(tab_ref[r], u_ref[...], preferred_element_type=F32)
        re.append(g[:quarter])
        im.append(g[quarter:])
    z = (
        (re[0] + re[1] + re[2] + re[3], im[0] + im[1] + im[2] + im[3]),
        (re[0] + im[1] - re[2] - im[3], im[0] - re[1] - im[2] + re[3]),
        (re[0] - re[1] + re[2] - re[3], im[0] - im[1] + im[2] - im[3]),
        (re[0] - im[1] - re[2] + im[3], im[0] + re[1] - im[2] - re[3]),
    )
    for q, (zr, zi) in enumerate(z):
        zc = jnp.concatenate([zr, zi], axis=1).astype(BF16)
        real = jnp.dot(zc, chan_ref[...], preferred_element_type=F32).astype(BF16)
        y = jnp.dot(real, wf_ref[0], preferred_element_type=F32)
        y_ref[0, q * quarter:(q + 1) * quarter, :] = y.astype(y_ref.dtype)


def _fourier_mix(uf, w_fourier):
    batch, radix, quarter, fw = uf.shape
    assert radix == DFT_RADIX
    seq = radix * quarter
    n_groups = w_fourier.shape[0]
    group_dim = fw // n_groups
    seq_tab, chan_tab = _dft_tables(seq, group_dim)
    seq_tab = jnp.asarray(seq_tab).astype(BF16)
    chan_tab = jnp.asarray(chan_tab).astype(BF16)

    def u_spec(r):
        return pl.BlockSpec((None, None, quarter, group_dim), lambda b, g, r=r: (b, r, 0, g))

    return pl.pallas_call(
        _fourier_kernel,
        grid=(batch, n_groups),
        in_specs=[
            _resident((DFT_RADIX, 2 * quarter, quarter), lambda b, g: (0, 0, 0)),
            u_spec(0), u_spec(1), u_spec(2), u_spec(3),
            _resident((2 * group_dim, group_dim), lambda b, g: (0, 0)),
            pl.BlockSpec((1, group_dim, group_dim), lambda b, g: (g, 0, 0)),
        ],
        out_specs=pl.BlockSpec((1, seq, group_dim), lambda b, g: (b, 0, g)),
        out_shape=jax.ShapeDtypeStruct((batch, seq, fw), BF16),
        compiler_params=_params(("arbitrary", "arbitrary")),
        name="fourier_mix",
    )(seq_tab, uf, uf, uf, uf, chan_tab, w_fourier)


def _attention_kernel(slopes_ref,
                      q1_ref, k1_ref, v1_ref, q4_ref, k4_ref, v4_ref, q16_ref, k16_ref, v16_ref,
                      o_ref,
                      bias_scr, o4_scr, lse4_scr, o16_scr, lse16_scr):
    seq = o_ref.shape[0]
    slope = slopes_ref[pl.program_id(1)]

    row = lax.broadcasted_iota(jnp.int32, (Q_BLOCK, K_WINDOW), 0)
    col = lax.broadcasted_iota(jnp.int32, (Q_BLOCK, K_WINDOW), 1)
    for gi, dil in enumerate(DILATIONS):
        for oi in range(3):
            rel = jnp.abs(col - row - oi * RADIUS)
            penalty = (slope * LOG2_E) * (dil * rel).astype(F32)
            bias_scr[gi * 3 + oi] = jnp.where(rel <= RADIUS, -penalty, MASK_VALUE)

    ones = jnp.ones((K_WINDOW, HEAD_DIM), BF16)

    def block(gi, dil, q_ref, k_ref, v_ref, idx):
        members = seq // dil
        blocks_per_class = members // Q_BLOCK
        cls = lax.shift_right_logical(idx, blocks_per_class.bit_length() - 1)
        blk = lax.bitwise_and(idx, blocks_per_class - 1)
        qs = pl.multiple_of(blk * Q_BLOCK, Q_BLOCK)
        ws = pl.multiple_of(jnp.clip(qs - RADIUS, 0, members - K_WINDOW), RADIUS)
        bias = bias_scr[gi * 3 + (qs - ws) // RADIUS]
        q = q_ref[cls, pl.ds(qs, Q_BLOCK), :]
        k = k_ref[cls, pl.ds(ws, K_WINDOW), :]
        v = v_ref[cls, pl.ds(ws, K_WINDOW), :]
        s = lax.dot_general(q, k, (((1,), (1,)), ((), ())), preferred_element_type=F32) + bias
        m = jnp.max(s, axis=1, keepdims=True)
        p = jnp.exp2(s - m).astype(BF16)
        acc = jnp.dot(p, jnp.concatenate([v, ones], axis=1), preferred_element_type=F32)
        den = acc[:, HEAD_DIM:]
        o = acc[:, :HEAD_DIM] * pl.reciprocal(den, approx=True)
        return qs * dil + cls, o, m + jnp.log2(den)

    def run_branch(gi, q_ref, k_ref, v_ref, emit):
        dil = DILATIONS[gi]

        def body(it, carry):
            for u in range(BLOCKS_PER_STEP):
                emit(dil, *block(gi, dil, q_ref, k_ref, v_ref, it * BLOCKS_PER_STEP + u))
            return carry

        lax.fori_loop(0, seq // Q_BLOCK // BLOCKS_PER_STEP, body, 0)

    def scatter_to(o_scr, lse_scr):
        def emit(dil, start, o, lse):
            rows = pl.ds(start, Q_BLOCK, stride=dil)
            o_scr[rows, :] = o
            lse_scr[rows, :] = lse
        return emit

    def merge(dil, start, o1, lse1):
        rows = pl.ds(pl.multiple_of(start, Q_BLOCK), Q_BLOCK)
        lse4, lse16 = lse4_scr[rows, :], lse16_scr[rows, :]
        top = jnp.maximum(lse1, jnp.maximum(lse4, lse16))
        e1, e4, e16 = jnp.exp2(lse1 - top), jnp.exp2(lse4 - top), jnp.exp2(lse16 - top)
        num = e1 * o1 + e4 * o4_scr[rows, :] + e16 * o16_scr[rows, :]
        o_ref[rows, :] = (num * pl.reciprocal(e1 + e4 + e16, approx=True)).astype(o_ref.dtype)

    run_branch(2, q16_ref, k16_ref, v16_ref, scatter_to(o16_scr, lse16_scr))
    run_branch(1, q4_ref, k4_ref, v4_ref, scatter_to(o4_scr, lse4_scr))
    run_branch(0, q1_ref, k1_ref, v1_ref, merge)


def _attention(heads_by_dilation, n_heads):
    batch, _, _, seq, _ = heads_by_dilation[0].shape
    slopes = jnp.asarray(2.0 ** (-8.0 * (np.arange(n_heads) + 1) / n_heads), dtype=F32)
    operands, in_specs = [], []
    for dil, heads in zip(DILATIONS, heads_by_dilation):
        for which in range(3):
            operands.append(heads)
            in_specs.append(pl.BlockSpec((None, None, dil, seq // dil, HEAD_DIM),
                                         lambda b, h, which=which: (b, which * n_heads + h, 0, 0, 0)))
    per_position = lambda: pltpu.VMEM((seq, HEAD_DIM), F32)
    return pl.pallas_call(
        _attention_kernel,
        grid=(batch, n_heads),
        in_specs=[pl.BlockSpec(memory_space=pltpu.SMEM)] + in_specs,
        out_specs=pl.BlockSpec((None, seq, HEAD_DIM), lambda b, h: (b, 0, h)),
        out_shape=jax.ShapeDtypeStruct((batch, seq, n_heads * HEAD_DIM), BF16),
        scratch_shapes=[pltpu.VMEM((3 * len(DILATIONS), Q_BLOCK, K_WINDOW), F32),
                        per_position(), per_position(), per_position(), per_position()],
        compiler_params=_params(("arbitrary", "arbitrary")),
        name="dilated_attention",
    )(slopes, *operands)


def _out_proj_kernel(yf_ref, ya_ref, x_ref, w_ref, h_ref):
    fw = yf_ref.shape[1]
    acc = jnp.dot(yf_ref[...], w_ref[:fw, :], preferred_element_type=F32)
    acc = acc + jnp.dot(ya_ref[...], w_ref[fw:, :], preferred_element_type=F32)
    h_ref[...] = x_ref[...] + acc


def _out_proj(yf, ya, x2, w_out, block_m):
    n_tok, d_model = x2.shape
    fw, aw = yf.shape[1], ya.shape[1]
    return pl.pallas_call(
        _out_proj_kernel,
        grid=(n_tok // block_m,),
        in_specs=[
            pl.BlockSpec((block_m, fw), lambda i: (i, 0)),
            pl.BlockSpec((block_m, aw), lambda i: (i, 0)),
            pl.BlockSpec((block_m, d_model), lambda i: (i, 0)),
            _resident((fw + aw, d_model), lambda i: (0, 0)),
        ],
        out_specs=pl.BlockSpec((block_m, d_model), lambda i: (i, 0)),
        out_shape=jax.ShapeDtypeStruct((n_tok, d_model), F32),
        compiler_params=_params(("arbitrary",)),
        name="out_proj",
    )(yf, ya, x2, w_out)


def _mlp_kernel(h_ref, g_ref, wu_ref, wd_ref, gf_ref, o_ref, u_scr, *, row_chunk):
    f = pl.program_id(1)
    last = pl.num_programs(1) - 1

    @pl.when(f == 0)
    def _():
        u_scr[...] = _rmsnorm_f32(h_ref[...], g_ref[...]).astype(BF16)

    def contribution(rows):
        a = jnp.dot(u_scr[rows, :], wu_ref[...], preferred_element_type=F32)
        a = jnp.square(jnp.maximum(a, 0.0)).astype(BF16)
        return jnp.dot(a, wd_ref[...], preferred_element_type=F32)

    chunks = [slice(c, c + row_chunk) for c in range(0, o_ref.shape[0], row_chunk)]

    @pl.when(f == 0)
    def _():
        for rows in chunks:
            o_ref[rows, :] = h_ref[rows, :] + contribution(rows)

    @pl.when(jnp.logical_and(f > 0, f < last))
    def _():
        for rows in chunks:
            o_ref[rows, :] += contribution(rows)

    @pl.when(f == last)
    def _():
        for rows in chunks:
            o_ref[rows, :] = _rmsnorm_f32(o_ref[rows, :] + contribution(rows), gf_ref[...])


def _mlp(h, g_mlp, w_up, w_down, g_final, block_m, block_f, row_chunk):
    n_tok, d_model = h.shape
    d_ff = w_up.shape[1]
    kernel = functools.partial(_mlp_kernel, row_chunk=row_chunk)
    return pl.pallas_call(
        kernel,
        grid=(n_tok // block_m, d_ff // block_f),
        in_specs=[
            pl.BlockSpec((block_m, d_model), lambda i, f: (i, 0)),
            _resident((1, d_model), lambda i, f: (0, 0)),
            pl.BlockSpec((d_model, block_f), lambda i, f: (0, f)),
            pl.BlockSpec((block_f, d_model), lambda i, f: (f, 0)),
            _resident((1, d_model), lambda i, f: (0, 0)),
        ],
        out_specs=pl.BlockSpec((block_m, d_model), lambda i, f: (i, 0)),
        out_shape=jax.ShapeDtypeStruct((n_tok, d_model), F32),
        scratch_shapes=[pltpu.VMEM((block_m, d_model), BF16)],
        compiler_params=_params(("arbitrary", "arbitrary")),
        name="mlp",
    )(h, g_mlp, w_up, w_down, g_final)


def kernel(x, norm_mix_g, w_in, w_fourier, w_out, norm_mlp_g, w_up, w_down, norm_final_g):
    batch, seq, d_model = x.shape
    depth = w_in.shape[0]
    fourier_width = w_fourier.shape[1] * w_fourier.shape[2]
    n_heads = (w_in.shape[2] - fourier_width) // (3 * HEAD_DIM)
    assert w_in.shape[2] == 4 * fourier_width and n_heads * HEAD_DIM == fourier_width
    assert seq % (max(DILATIONS) * K_WINDOW) == 0
    assert depth == 1, "the MLP kernel fuses the final RMSNorm, so it handles a single layer"

    h = x.reshape(batch * seq, d_model)
    uf, *heads_by_dilation = _in_proj(h, norm_mix_g[0][None, :], w_in[0].astype(BF16),
                                      batch, seq, block_m=1024)
    yf = _fourier_mix(uf, w_fourier[0].astype(BF16))
    ya = _attention(heads_by_dilation, n_heads)
    h = _out_proj(yf.reshape(batch * seq, fourier_width),
                  ya.reshape(batch * seq, n_heads * HEAD_DIM),
                  h, w_out[0].astype(BF16), block_m=512)
    h = _mlp(h, norm_mlp_g[0][None, :], w_up[0].astype(BF16), w_down[0].astype(BF16),
             norm_final_g[None, :], block_m=1024, block_f=512, row_chunk=512)
    return h.reshape(batch, seq, d_model)
```

```python
import functools

import numpy as np
import jax
import jax.numpy as jnp
from jax import lax
from jax.experimental import pallas as pl
from jax.experimental.pallas import tpu as pltpu

F32 = jnp.float32
BF16 = jnp.bfloat16

HEAD_DIM = 128
WINDOW_DILATIONS = ((128, 1), (512, 4), (2048, 16))
DILATIONS = tuple(d for _, d in WINDOW_DILATIONS)
RADIUS = 64
assert all((w // 2) // d == RADIUS for w, d in WINDOW_DILATIONS)
RMS_EPS = 1e-6
MASK_VALUE = -1e30
LOG2_E = 1.4426950408889634

Q_BLOCK = 128
K_WINDOW = Q_BLOCK + 2 * RADIUS
BLOCKS_PER_STEP = 8
DFT_RADIX = 4

VMEM_LIMIT_BYTES = 60 * 1024 * 1024


def _params(semantics):
    return pltpu.CompilerParams(dimension_semantics=semantics, vmem_limit_bytes=VMEM_LIMIT_BYTES)


def _resident(block_shape, index_map):
    return pl.BlockSpec(block_shape, index_map, pipeline_mode=pl.Buffered(1))


def _rmsnorm_f32(x, g):
    inv = lax.rsqrt(jnp.mean(x * x, axis=-1, keepdims=True) + RMS_EPS)
    return x * inv * g


def _in_proj_kernel(x_ref, g_ref, w_ref, nat_ref, c4_ref, c16_ref, u_scr, nat_scr, c4_scr, *,
                    heads_per_block, q_scale):
    j = pl.program_id(1)
    block_m = u_scr.shape[0]
    chunk_heads = 2
    chunk_n = chunk_heads * HEAD_DIM

    @pl.when(j == 0)
    def _():
        u_scr[...] = _rmsnorm_f32(x_ref[...], g_ref[...]).astype(BF16)

    scale = jnp.where(j == 1, q_scale, 1.0)
    for c in range(heads_per_block // chunk_heads):
        acc = jnp.dot(u_scr[...], w_ref[:, c * chunk_n:(c + 1) * chunk_n],
                      preferred_element_type=F32) * scale
        for hh in range(c * chunk_heads, (c + 1) * chunk_heads):
            head = acc[:, (hh % chunk_heads) * HEAD_DIM:(hh % chunk_heads + 1) * HEAD_DIM]
            nat_ref[0, hh, 0] = head.astype(BF16)
            nat_scr[hh] = head
            for r in range(4):
                piece = nat_scr[hh, pl.ds(r, block_m // 4, stride=4), :]
                c4_scr[hh, r] = piece
                c4_ref[0, hh, r] = piece.astype(BF16)
            for r in range(4):
                for a in range(4):
                    piece = c4_scr[hh, r, pl.ds(a, block_m // 16, stride=4), :]
                    c16_ref[0, hh, 4 * a + r] = piece.astype(BF16)


def _in_proj(x2, g, w_in, batch, seq, block_m):
    n_tok, d_model = x2.shape
    in_width = w_in.shape[1]
    block_n = in_width // 4
    heads_per_block = block_n // HEAD_DIM
    tiles_per_batch = seq // block_m
    kernel = functools.partial(_in_proj_kernel, heads_per_block=heads_per_block,
                               q_scale=HEAD_DIM ** -0.5 * LOG2_E)

    def head_block(dil):
        return pl.BlockSpec((1, heads_per_block, dil, block_m // dil, HEAD_DIM),
                            lambda i, j: (i // tiles_per_batch, j, 0, i % tiles_per_batch, 0))

    def head_shape(dil):
        return jax.ShapeDtypeStruct((batch, in_width // HEAD_DIM, dil, seq // dil, HEAD_DIM), BF16)

    return pl.pallas_call(
        kernel,
        grid=(n_tok // block_m, in_width // block_n),
        in_specs=[
            pl.BlockSpec((block_m, d_model), lambda i, j: (i, 0)),
            _resident((1, d_model), lambda i, j: (0, 0)),
            pl.BlockSpec((d_model, block_n), lambda i, j: (0, j)),
        ],
        out_specs=[head_block(dil) for dil in DILATIONS],
        out_shape=[head_shape(dil) for dil in DILATIONS],
        scratch_shapes=[pltpu.VMEM((block_m, d_model), BF16),
                        pltpu.VMEM((heads_per_block, block_m, HEAD_DIM), F32),
                        pltpu.VMEM((heads_per_block, 4, block_m // 4, HEAD_DIM), F32)],
        compiler_params=_params(("arbitrary", "arbitrary")),
        name="in_proj",
    )(x2, g, w_in)


def _dft_tables(seq, group_dim):
    quarter = seq // DFT_RADIX
    s = np.arange(quarter, dtype=np.int64)[:, None]
    tabs = []
    for r in range(DFT_RADIX):
        t = DFT_RADIX * np.arange(quarter, dtype=np.int64)[None, :] + r
        ang = 2.0 * np.pi * ((s * t) % seq).astype(np.float64) / seq
        tabs.append(np.concatenate([np.cos(ang), -np.sin(ang)], axis=0))
    seq_tab = np.stack(tabs).astype(np.float32)
    c = np.arange(group_dim, dtype=np.int64)
    ang = 2.0 * np.pi * ((c[:, None] * c[None, :]) % group_dim).astype(np.float64) / group_dim
    ortho = 1.0 / np.sqrt(float(seq) * float(group_dim))
    chan_tab = (np.concatenate([np.cos(ang), np.sin(ang)], axis=0) * ortho).astype(np.float32)
    return seq_tab, chan_tab


def _fourier_kernel(tab_ref, u_ref, chan_ref, wf_ref, y_ref):
    slabs, _, quarter, _ = u_ref.shape
    re, im = [], []
    for r in range(DFT_RADIX):
        u = jnp.concatenate([u_ref[i, r] for i in range(slabs)], axis=1)
        g = jnp.dot(tab_ref[r], u, preferred_element_type=F32)
        re.append(g[:quarter])
        im.append(g[quarter:])
    z = (
        (re[0] + re[1] + re[2] + re[3], im[0] + im[1] + im[2] + im[3]),
        (re[0] + im[1] - re[2] - im[3], im[0] - re[1] - im[2] + re[3]),
        (re[0] - re[1] + re[2] - re[3], im[0] - im[1] + im[2] - im[3]),
        (re[0] - im[1] - re[2] + im[3], im[0] + re[1] - im[2] - re[3]),
    )
    for q, (zr, zi) in enumerate(z):
        zc = jnp.concatenate([zr, zi], axis=1).astype(BF16)
        real = jnp.dot(zc, chan_ref[...], preferred_element_type=F32).astype(BF16)
        y = jnp.dot(real, wf_ref[0], preferred_element_type=F32)
        y_ref[0, q * quarter:(q + 1) * quarter, :] = y.astype(y_ref.dtype)


def _fourier_mix(heads4, w_fourier):
    batch, _, radix, quarter, _ = heads4.shape
    assert radix == DFT_RADIX
    seq = radix * quarter
    n_groups, group_dim, _ = w_fourier.shape
    fw = n_groups * group_dim
    slabs = group_dim // HEAD_DIM
    seq_tab, chan_tab = _dft_tables(seq, group_dim)
    seq_tab = jnp.asarray(seq_tab).astype(BF16)
    chan_tab = jnp.asarray(chan_tab).astype(BF16)

    return pl.pallas_call(
        _fourier_kernel,
        grid=(batch, n_groups),
        in_specs=[
            _resident((DFT_RADIX, 2 * quarter, quarter), lambda b, g: (0, 0, 0)),
            pl.BlockSpec((None, slabs, DFT_RADIX, quarter, HEAD_DIM), lambda b, g: (b, g, 0, 0, 0)),
            _resident((2 * group_dim, group_dim), lambda b, g: (0, 0)),
            pl.BlockSpec((1, group_dim, group_dim), lambda b, g: (g, 0, 0)),
        ],
        out_specs=pl.BlockSpec((1, seq, group_dim), lambda b, g: (b, 0, g)),
        out_shape=jax.ShapeDtypeStruct((batch, seq, fw), BF16),
        compiler_params=_params(("arbitrary", "arbitrary")),
        name="fourier_mix",
    )(seq_tab, heads4, chan_tab, w_fourier)


def _attention_kernel(slopes_ref,
                      q1_ref, k1_ref, v1_ref, q4_ref, k4_ref, v4_ref, q16_ref, k16_ref, v16_ref,
                      o_ref,
                      bias_scr, o4_scr, lse4_scr, o16_scr, lse16_scr):
    seq = o_ref.shape[0]
    slope = slopes_ref[pl.program_id(1)]

    row = lax.broadcasted_iota(jnp.int32, (Q_BLOCK, K_WINDOW), 0)
    col = lax.broadcasted_iota(jnp.int32, (Q_BLOCK, K_WINDOW), 1)
    for gi, dil in enumerate(DILATIONS):
        for oi in range(3):
            rel = jnp.abs(col - row - oi * RADIUS)
            penalty = (slope * LOG2_E) * (dil * rel).astype(F32)
            bias_scr[gi * 3 + oi] = jnp.where(rel <= RADIUS, -penalty, MASK_VALUE)

    ones = jnp.ones((K_WINDOW, HEAD_DIM), BF16)

    def block(gi, dil, q_ref, k_ref, v_ref, idx):
        members = seq // dil
        blocks_per_class = members // Q_BLOCK
        cls = lax.shift_right_logical(idx, blocks_per_class.bit_length() - 1)
        blk = lax.bitwise_and(idx, blocks_per_class - 1)
        qs = pl.multiple_of(blk * Q_BLOCK, Q_BLOCK)
        ws = pl.multiple_of(jnp.clip(qs - RADIUS, 0, members - K_WINDOW), RADIUS)
        bias = bias_scr[gi * 3 + (qs - ws) // RADIUS]
        q = q_ref[cls, pl.ds(qs, Q_BLOCK), :]
        k = k_ref[cls, pl.ds(ws, K_WINDOW), :]
        v = v_ref[cls, pl.ds(ws, K_WINDOW), :]
        s = lax.dot_general(q, k, (((1,), (1,)), ((), ())), preferred_element_type=F32) + bias
        m = jnp.max(s, axis=1, keepdims=True)
        p = jnp.exp2(s - m).astype(BF16)
        acc = jnp.dot(p, jnp.concatenate([v, ones], axis=1), preferred_element_type=F32)
        den = acc[:, HEAD_DIM:]
        o = acc[:, :HEAD_DIM] * pl.reciprocal(den, approx=True)
        return qs * dil + cls, o, m + jnp.log2(den)

    def run_branch(gi, q_ref, k_ref, v_ref, emit):
        dil = DILATIONS[gi]

        def body(it, carry):
            for u in range(BLOCKS_PER_STEP):
                emit(dil, *block(gi, dil, q_ref, k_ref, v_ref, it * BLOCKS_PER_STEP + u))
            return carry

        lax.fori_loop(0, seq // Q_BLOCK // BLOCKS_PER_STEP, body, 0)

    def scatter_to(o_scr, lse_scr):
        def emit(dil, start, o, lse):
            rows = pl.ds(start, Q_BLOCK, stride=dil)
            o_scr[rows, :] = o
            lse_scr[rows, :] = lse
        return emit

    def merge(dil, start, o1, lse1):
        rows = pl.ds(pl.multiple_of(start, Q_BLOCK), Q_BLOCK)
        lse4, lse16 = lse4_scr[rows, :], lse16_scr[rows, :]
        top = jnp.maximum(lse1, jnp.maximum(lse4, lse16))
        e1, e4, e16 = jnp.exp2(lse1 - top), jnp.exp2(lse4 - top), jnp.exp2(lse16 - top)
        num = e1 * o1 + e4 * o4_scr[rows, :] + e16 * o16_scr[rows, :]
        o_ref[rows, :] = (num * pl.reciprocal(e1 + e4 + e16, approx=True)).astype(o_ref.dtype)

    run_branch(2, q16_ref, k16_ref, v16_ref, scatter_to(o16_scr, lse16_scr))
    run_branch(1, q4_ref, k4_ref, v4_ref, scatter_to(o4_scr, lse4_scr))
    run_branch(0, q1_ref, k1_ref, v1_ref, merge)


def _attention(heads_by_dilation, n_heads):
    batch, n_all, _, seq, _ = heads_by_dilation[0].shape
    first = n_all - 3 * n_heads
    slopes = jnp.asarray(2.0 ** (-8.0 * (np.arange(n_heads) + 1) / n_heads), dtype=F32)
    operands, in_specs = [], []
    for dil, heads in zip(DILATIONS, heads_by_dilation):
        for which in range(3):
            operands.append(heads)
            in_specs.append(pl.BlockSpec((None, None, dil, seq // dil, HEAD_DIM),
                                         lambda b, h, which=which:
                                         (b, first + which * n_heads + h, 0, 0, 0)))
    per_position = lambda: pltpu.VMEM((seq, HEAD_DIM), F32)
    return pl.pallas_call(
        _attention_kernel,
        grid=(batch, n_heads),
        in_specs=[pl.BlockSpec(memory_space=pltpu.SMEM)] + in_specs,
        out_specs=pl.BlockSpec((None, seq, HEAD_DIM), lambda b, h: (b, 0, h)),
        out_shape=jax.ShapeDtypeStruct((batch, seq, n_heads * HEAD_DIM), BF16),
        scratch_shapes=[pltpu.VMEM((3 * len(DILATIONS), Q_BLOCK, K_WINDOW), F32),
                        per_position(), per_position(), per_position(), per_position()],
        compiler_params=_params(("arbitrary", "arbitrary")),
        name="dilated_attention",
    )(slopes, *operands)


def _out_proj_kernel(yf_ref, ya_ref, x_ref, w_ref, h_ref):
    fw = yf_ref.shape[1]
    acc = jnp.dot(yf_ref[...], w_ref[:fw, :], preferred_element_type=F32)
    acc = acc + jnp.dot(ya_ref[...], w_ref[fw:, :], preferred_element_type=F32)
    h_ref[...] = x_ref[...] + acc


def _out_proj(yf, ya, x2, w_out, block_m):
    n_tok, d_model = x2.shape
    fw, aw = yf.shape[1], ya.shape[1]
    return pl.pallas_call(
        _out_proj_kernel,
        grid=(n_tok // block_m,),
        in_specs=[
            pl.BlockSpec((block_m, fw), lambda i: (i, 0)),
            pl.BlockSpec((block_m, aw), lambda i: (i, 0)),
            pl.BlockSpec((block_m, d_model), lambda i: (i, 0)),
            _resident((fw + aw, d_model), lambda i: (0, 0)),
        ],
        out_specs=pl.BlockSpec((block_m, d_model), lambda i: (i, 0)),
        out_shape=jax.ShapeDtypeStruct((n_tok, d_model), F32),
        compiler_params=_params(("arbitrary",)),
        name="out_proj",
    )(yf, ya, x2, w_out)


def _mlp_kernel(h_ref, g_ref, wu_ref, wd_ref, gf_ref, o_ref, u_scr, *, row_chunk):
    f = pl.program_id(1)
    last = pl.num_programs(1) - 1

    @pl.when(f == 0)
    def _():
        u_scr[...] = _rmsnorm_f32(h_ref[...], g_ref[...]).astype(BF16)

    def contribution(rows):
        a = jnp.dot(u_scr[rows, :], wu_ref[...], preferred_element_type=F32)
        a = jnp.square(jnp.maximum(a, 0.0)).astype(BF16)
        return jnp.dot(a, wd_ref[...], preferred_element_type=F32)

    chunks = [slice(c, c + row_chunk) for c in range(0, o_ref.shape[0], row_chunk)]

    @pl.when(f == 0)
    def _():
        for rows in chunks:
            o_ref[rows, :] = h_ref[rows, :] + contribution(rows)

    @pl.when(jnp.logical_and(f > 0, f < last))
    def _():
        for rows in chunks:
            o_ref[rows, :] += contribution(rows)

    @pl.when(f == last)
    def _():
        for rows in chunks:
            o_ref[rows, :] = _rmsnorm_f32(o_ref[rows, :] + contribution(rows), gf_ref[...])


def _mlp(h, g_mlp, w_up, w_down, g_final, block_m, block_f, row_chunk):
    n_tok, d_model = h.shape
    d_ff = w_up.shape[1]
    kernel = functools.partial(_mlp_kernel, row_chunk=row_chunk)
    return pl.pallas_call(
        kernel,
        grid=(n_tok // block_m, d_ff // block_f),
        in_specs=[
            pl.BlockSpec((block_m, d_model), lambda i, f: (i, 0)),
            _resident((1, d_model), lambda i, f: (0, 0)),
            pl.BlockSpec((d_model, block_f), lambda i, f: (0, f)),
            pl.BlockSpec((block_f, d_model), lambda i, f: (f, 0)),
            _resident((1, d_model), lambda i, f: (0, 0)),
        ],
        out_specs=pl.BlockSpec((block_m, d_model), lambda i, f: (i, 0)),
        out_shape=jax.ShapeDtypeStruct((n_tok, d_model), F32),
        scratch_shapes=[pltpu.VMEM((block_m, d_model), BF16)],
        compiler_params=_params(("arbitrary", "arbitrary")),
        name="mlp",
    )(h, g_mlp, w_up, w_down, g_final)


def kernel(x, norm_mix_g, w_in, w_fourier, w_out, norm_mlp_g, w_up, w_down, norm_final_g):
    batch, seq, d_model = x.shape
    depth = w_in.shape[0]
    fourier_width = w_fourier.shape[1] * w_fourier.shape[2]
    n_heads = (w_in.shape[2] - fourier_width) // (3 * HEAD_DIM)
    assert w_in.shape[2] == 4 * fourier_width and n_heads * HEAD_DIM == fourier_width
    assert seq % (max(DILATIONS) * K_WINDOW) == 0
    assert depth == 1, "the MLP kernel fuses the final RMSNorm, so it handles a single layer"

    h = x.reshape(batch * seq, d_model)
    heads_by_dilation = _in_proj(h, norm_mix_g[0][None, :], w_in[0].astype(BF16),
                                 batch, seq, block_m=1024)
    yf = _fourier_mix(heads_by_dilation[DILATIONS.index(DFT_RADIX)], w_fourier[0].astype(BF16))
    ya = _attention(heads_by_dilation, n_heads)
    h = _out_proj(yf.reshape(batch * seq, fourier_width),
                  ya.reshape(batch * seq, n_heads * HEAD_DIM),
                  h, w_out[0].astype(BF16), block_m=512)
    h = _mlp(h, norm_mlp_g[0][None, :], w_up[0].astype(BF16), w_down[0].astype(BF16),
             norm_final_g[None, :], block_m=1024, block_f=512, row_chunk=512)
    return h.reshape(batch, seq, d_model)
```

```python
import functools

import numpy as np
import jax
import jax.numpy as jnp
from jax import lax
from jax.experimental import pallas as pl
from jax.experimental.pallas import tpu as pltpu

F32 = jnp.float32
BF16 = jnp.bfloat16

HEAD_DIM = 128
WINDOW_DILATIONS = ((128, 1), (512, 4), (2048, 16))
DILATIONS = tuple(d for _, d in WINDOW_DILATIONS)
RADIUS = 64
assert all((w // 2) // d == RADIUS for w, d in WINDOW_DILATIONS)
RMS_EPS = 1e-6
MASK_VALUE = -1e30
LOG2_E = 1.4426950408889634

Q_BLOCK = 128
K_WINDOW = Q_BLOCK + 2 * RADIUS
DFT_RADIX = 4

VMEM_LIMIT_BYTES = 60 * 1024 * 1024


def _params(semantics):
    return pltpu.CompilerParams(dimension_semantics=semantics, vmem_limit_bytes=VMEM_LIMIT_BYTES)


def _resident(block_shape, index_map):
    return pl.BlockSpec(block_shape, index_map, pipeline_mode=pl.Buffered(1))


def _rmsnorm_f32(x, g):
    inv = lax.rsqrt(jnp.mean(x * x, axis=-1, keepdims=True) + RMS_EPS)
    return x * inv * g


def _in_proj_kernel(x_ref, g_ref, w_ref, nat_ref, c4_ref, c16_ref, u_scr, nat_scr, c4_scr, *,
                    heads_per_block, q_scale):
    j = pl.program_id(1)
    block_m = u_scr.shape[0]
    chunk_heads = 2
    chunk_n = chunk_heads * HEAD_DIM

    @pl.when(j == 0)
    def _():
        u_scr[...] = _rmsnorm_f32(x_ref[...], g_ref[...]).astype(BF16)

    scale = jnp.where(j == 1, q_scale, 1.0)
    for c in range(heads_per_block // chunk_heads):
        acc = jnp.dot(u_scr[...], w_ref[:, c * chunk_n:(c + 1) * chunk_n],
                      preferred_element_type=F32) * scale
        for hh in range(c * chunk_heads, (c + 1) * chunk_heads):
            head = acc[:, (hh % chunk_heads) * HEAD_DIM:(hh % chunk_heads + 1) * HEAD_DIM]
            nat_ref[0, hh, 0] = head.astype(BF16)
            nat_scr[hh] = head
            for r in range(4):
                piece = nat_scr[hh, pl.ds(r, block_m // 4, stride=4), :]
                c4_scr[hh, r] = piece
                c4_ref[0, hh, r] = piece.astype(BF16)
            for r in range(4):
                for a in range(4):
                    piece = c4_scr[hh, r, pl.ds(a, block_m // 16, stride=4), :]
                    c16_ref[0, hh, 4 * a + r] = piece.astype(BF16)


def _in_proj(x2, g, w_in, batch, seq, block_m):
    n_tok, d_model = x2.shape
    in_width = w_in.shape[1]
    block_n = in_width // 4
    heads_per_block = block_n // HEAD_DIM
    tiles_per_batch = seq // block_m
    kernel = functools.partial(_in_proj_kernel, heads_per_block=heads_per_block,
                               q_scale=HEAD_DIM ** -0.5 * LOG2_E)

    def head_block(dil):
        return pl.BlockSpec((1, heads_per_block, dil, block_m // dil, HEAD_DIM),
                            lambda i, j: (i // tiles_per_batch, j, 0, i % tiles_per_batch, 0))

    def head_shape(dil):
        return jax.ShapeDtypeStruct((batch, in_width // HEAD_DIM, dil, seq // dil, HEAD_DIM), BF16)

    return pl.pallas_call(
        kernel,
        grid=(n_tok // block_m, in_width // block_n),
        in_specs=[
            pl.BlockSpec((block_m, d_model), lambda i, j: (i, 0)),
            _resident((1, d_model), lambda i, j: (0, 0)),
            pl.BlockSpec((d_model, block_n), lambda i, j: (0, j)),
        ],
        out_specs=[head_block(dil) for dil in DILATIONS],
        out_shape=[head_shape(dil) for dil in DILATIONS],
        scratch_shapes=[pltpu.VMEM((block_m, d_model), BF16),
                        pltpu.VMEM((heads_per_block, block_m, HEAD_DIM), F32),
                        pltpu.VMEM((heads_per_block, 4, block_m // 4, HEAD_DIM), F32)],
        compiler_params=_params(("arbitrary", "arbitrary")),
        name="in_proj",
    )(x2, g, w_in)


def _dft_tables(seq, group_dim):
    quarter = seq // DFT_RADIX
    s = np.arange(quarter, dtype=np.int64)[:, None]
    tabs = []
    for r in range(DFT_RADIX):
        t = DFT_RADIX * np.arange(quarter, dtype=np.int64)[None, :] + r
        ang = 2.0 * np.pi * ((s * t) % seq).astype(np.float64) / seq
        tabs.append(np.concatenate([np.cos(ang), -np.sin(ang)], axis=0))
    seq_tab = np.stack(tabs).astype(np.float32)
    c = np.arange(group_dim, dtype=np.int64)
    ang = 2.0 * np.pi * ((c[:, None] * c[None, :]) % group_dim).astype(np.float64) / group_dim
    ortho = 1.0 / np.sqrt(float(seq) * float(group_dim))
    chan_tab = (np.concatenate([np.cos(ang), np.sin(ang)], axis=0) * ortho).astype(np.float32)
    return seq_tab, chan_tab


def _fourier_kernel(tab_ref, u_ref, chan_ref, wf_ref, y_ref):
    slabs, _, quarter, _ = u_ref.shape
    re, im = [], []
    for r in range(DFT_RADIX):
        u = jnp.concatenate([u_ref[i, r] for i in range(slabs)], axis=1)
        g = jnp.dot(tab_ref[r], u, preferred_element_type=F32)
        re.append(g[:quarter])
        im.append(g[quarter:])
    z = (
        (re[0] + re[1] + re[2] + re[3], im[0] + im[1] + im[2] + im[3]),
        (re[0] + im[1] - re[2] - im[3], im[0] - re[1] - im[2] + re[3]),
        (re[0] - re[1] + re[2] - re[3], im[0] - im[1] + im[2] - im[3]),
        (re[0] - im[1] - re[2] + im[3], im[0] + re[1] - im[2] - re[3]),
    )
    for q, (zr, zi) in enumerate(z):
        zc = jnp.concatenate([zr, zi], axis=1).astype(BF16)
        real = jnp.dot(zc, chan_ref[...], preferred_element_type=F32).astype(BF16)
        y = jnp.dot(real, wf_ref[0], preferred_element_type=F32)
        y_ref[0, q * quarter:(q + 1) * quarter, :] = y.astype(y_ref.dtype)


def _fourier_mix(heads4, w_fourier):
    batch, _, radix, quarter, _ = heads4.shape
    assert radix == DFT_RADIX
    seq = radix * quarter
    n_groups, group_dim, _ = w_fourier.shape
    fw = n_groups * group_dim
    slabs = group_dim // HEAD_DIM
    seq_tab, chan_tab = _dft_tables(seq, group_dim)
    seq_tab = jnp.asarray(seq_tab).astype(BF16)
    chan_tab = jnp.asarray(chan_tab).astype(BF16)

    return pl.pallas_call(
        _fourier_kernel,
        grid=(batch, n_groups),
        in_specs=[
            _resident((DFT_RADIX, 2 * quarter, quarter), lambda b, g: (0, 0, 0)),
            pl.BlockSpec((None, slabs, DFT_RADIX, quarter, HEAD_DIM), lambda b, g: (b, g, 0, 0, 0)),
            _resident((2 * group_dim, group_dim), lambda b, g: (0, 0)),
            pl.BlockSpec((1, group_dim, group_dim), lambda b, g: (g, 0, 0)),
        ],
        out_specs=pl.BlockSpec((1, seq, group_dim), lambda b, g: (b, 0, g)),
        out_shape=jax.ShapeDtypeStruct((batch, seq, fw), BF16),
        compiler_params=_params(("arbitrary", "arbitrary")),
        name="fourier_mix",
    )(seq_tab, heads4, chan_tab, w_fourier)


SCRATCH_PITCH = {1: 1, 4: 4, 16: 24}


def _attention_kernel(slopes_ref,
                      q1_ref, k1_ref, v1_ref, q4_ref, k4_ref, v4_ref, q16_ref, k16_ref, v16_ref,
                      o_ref,
                      bias_scr, acc4_scr, m4_scr, den4_scr, acc16_scr, m16_scr, den16_scr):
    seq = o_ref.shape[0]
    slope = slopes_ref[pl.program_id(1)]

    row = lax.broadcasted_iota(jnp.int32, (Q_BLOCK, K_WINDOW), 0)
    col = lax.broadcasted_iota(jnp.int32, (Q_BLOCK, K_WINDOW), 1)
    for gi, dil in enumerate(DILATIONS):
        for oi in range(3):
            rel = jnp.abs(col - row - oi * RADIUS)
            penalty = (slope * LOG2_E) * (dil * rel).astype(F32)
            bias_scr[gi * 3 + oi] = jnp.where(rel <= RADIUS, -penalty, MASK_VALUE)

    ones = jnp.ones((K_WINDOW, HEAD_DIM), BF16)
    branches = (
        (q1_ref, k1_ref, v1_ref, None, None, None),
        (q4_ref, k4_ref, v4_ref, acc4_scr, m4_scr, den4_scr),
        (q16_ref, k16_ref, v16_ref, acc16_scr, m16_scr, den16_scr),
    )

    def natural_rows(scr, gi, qs):
        dil, pitch = DILATIONS[gi], SCRATCH_PITCH[DILATIONS[gi]]
        if pitch == dil:
            return scr[qs:qs + Q_BLOCK, :]
        base = qs // dil * pitch
        return jnp.concatenate([scr[base + i * pitch:base + i * pitch + dil, :]
                                for i in range(Q_BLOCK // dil)], axis=0)

    def block(gi, cls, qs):
        q_ref, k_ref, v_ref, acc_scr, m_scr, den_scr = branches[gi]
        members = seq // DILATIONS[gi]
        ws = min(max(qs - RADIUS, 0), members - K_WINDOW)
        q = q_ref[cls, qs:qs + Q_BLOCK, :]
        k = k_ref[cls, ws:ws + K_WINDOW, :]
        v = v_ref[cls, ws:ws + K_WINDOW, :]
        s = lax.dot_general(q, k, (((1,), (1,)), ((), ())), preferred_element_type=F32)
        s = s + bias_scr[gi * 3 + (qs - ws) // RADIUS]
        m = jnp.max(s, axis=1, keepdims=True)
        p = jnp.exp2(s - m).astype(BF16)
        acc = jnp.dot(p, jnp.concatenate([v, ones], axis=1), preferred_element_type=F32)
        num, den = acc[:, :HEAD_DIM], acc[:, HEAD_DIM:]
        m = jnp.broadcast_to(m, (Q_BLOCK, HEAD_DIM))
        if gi > 0:
            pitch = SCRATCH_PITCH[DILATIONS[gi]]
            rows = pl.ds(qs * pitch + cls, Q_BLOCK, stride=pitch)
            acc_scr[rows, :] = num
            m_scr[rows, :] = m
            den_scr[rows, :] = den
            return
        m4, m16 = natural_rows(m4_scr, 1, qs), natural_rows(m16_scr, 2, qs)
        top = jnp.maximum(m, jnp.maximum(m4, m16))
        e1, e4, e16 = jnp.exp2(m - top), jnp.exp2(m4 - top), jnp.exp2(m16 - top)
        num = e1 * num + e4 * natural_rows(acc4_scr, 1, qs) + e16 * natural_rows(acc16_scr, 2, qs)
        den = e1 * den + e4 * natural_rows(den4_scr, 1, qs) + e16 * natural_rows(den16_scr, 2, qs)
        o_ref[qs:qs + Q_BLOCK, :] = (num * pl.reciprocal(den, approx=True)).astype(o_ref.dtype)

    for gi in (2, 1, 0):
        members = seq // DILATIONS[gi]
        for cls in range(DILATIONS[gi]):
            for qs in range(0, members, Q_BLOCK):
                block(gi, cls, qs)


def _attention(heads_by_dilation, n_heads):
    batch, n_all, _, seq, _ = heads_by_dilation[0].shape
    first = n_all - 3 * n_heads
    slopes = jnp.asarray(2.0 ** (-8.0 * (np.arange(n_heads) + 1) / n_heads), dtype=F32)
    operands, in_specs = [], []
    for dil, heads in zip(DILATIONS, heads_by_dilation):
        for which in range(3):
            operands.append(heads)
            in_specs.append(pl.BlockSpec((None, None, dil, seq // dil, HEAD_DIM),
                                         lambda b, h, which=which:
                                         (b, first + which * n_heads + h, 0, 0, 0)))
    result = lambda dil: pltpu.VMEM((seq // dil * SCRATCH_PITCH[dil], HEAD_DIM), F32)
    return pl.pallas_call(
        _attention_kernel,
        grid=(batch, n_heads),
        in_specs=[pl.BlockSpec(memory_space=pltpu.SMEM)] + in_specs,
        out_specs=pl.BlockSpec((None, seq, HEAD_DIM), lambda b, h: (b, 0, h)),
        out_shape=jax.ShapeDtypeStruct((batch, seq, n_heads * HEAD_DIM), BF16),
        scratch_shapes=[pltpu.VMEM((3 * len(DILATIONS), Q_BLOCK, K_WINDOW), F32),
                        result(4), result(4), result(4), result(16), result(16), result(16)],
        compiler_params=_params(("arbitrary", "arbitrary")),
        name="dilated_attention",
    )(slopes, *operands)


def _out_proj_kernel(yf_ref, ya_ref, x_ref, w_ref, h_ref):
    fw = yf_ref.shape[1]
    acc = jnp.dot(yf_ref[...], w_ref[:fw, :], preferred_element_type=F32)
    acc = acc + jnp.dot(ya_ref[...], w_ref[fw:, :], preferred_element_type=F32)
    h_ref[...] = x_ref[...] + acc


def _out_proj(yf, ya, x2, w_out, block_m):
    n_tok, d_model = x2.shape
    fw, aw = yf.shape[1], ya.shape[1]
    return pl.pallas_call(
        _out_proj_kernel,
        grid=(n_tok // block_m,),
        in_specs=[
            pl.BlockSpec((block_m, fw), lambda i: (i, 0)),
            pl.BlockSpec((block_m, aw), lambda i: (i, 0)),
            pl.BlockSpec((block_m, d_model), lambda i: (i, 0)),
            _resident((fw + aw, d_model), lambda i: (0, 0)),
        ],
        out_specs=pl.BlockSpec((block_m, d_model), lambda i: (i, 0)),
        out_shape=jax.ShapeDtypeStruct((n_tok, d_model), F32),
        compiler_params=_params(("arbitrary",)),
        name="out_proj",
    )(yf, ya, x2, w_out)


def _mlp_kernel(h_ref, g_ref, wu_ref, wd_ref, gf_ref, o_ref, u_scr, *, row_chunk):
    f = pl.program_id(1)
    last = pl.num_programs(1) - 1

    @pl.when(f == 0)
    def _():
        u_scr[...] = _rmsnorm_f32(h_ref[...], g_ref[...]).astype(BF16)

    def contribution(rows):
        a = jnp.dot(u_scr[rows, :], wu_ref[...], preferred_element_type=F32)
        a = jnp.square(jnp.maximum(a, 0.0)).astype(BF16)
        return jnp.dot(a, wd_ref[...], preferred_element_type=F32)

    chunks = [slice(c, c + row_chunk) for c in range(0, o_ref.shape[0], row_chunk)]

    @pl.when(f == 0)
    def _():
        for rows in chunks:
            o_ref[rows, :] = h_ref[rows, :] + contribution(rows)

    @pl.when(jnp.logical_and(f > 0, f < last))
    def _():
        for rows in chunks:
            o_ref[rows, :] += contribution(rows)

    @pl.when(f == last)
    def _():
        for rows in chunks:
            o_ref[rows, :] = _rmsnorm_f32(o_ref[rows, :] + contribution(rows), gf_ref[...])


def _mlp(h, g_mlp, w_up, w_down, g_final, block_m, block_f, row_chunk):
    n_tok, d_model = h.shape
    d_ff = w_up.shape[1]
    kernel = functools.partial(_mlp_kernel, row_chunk=row_chunk)
    return pl.pallas_call(
        kernel,
        grid=(n_tok // block_m, d_ff // block_f),
        in_specs=[
            pl.BlockSpec((block_m, d_model), lambda i, f: (i, 0)),
            _resident((1, d_model), lambda i, f: (0, 0)),
            pl.BlockSpec((d_model, block_f), lambda i, f: (0, f)),
            pl.BlockSpec((block_f, d_model), lambda i, f: (f, 0)),
            _resident((1, d_model), lambda i, f: (0, 0)),
        ],
        out_specs=pl.BlockSpec((block_m, d_model), lambda i, f: (i, 0)),
        out_shape=jax.ShapeDtypeStruct((n_tok, d_model), F32),
        scratch_shapes=[pltpu.VMEM((block_m, d_model), BF16)],
        compiler_params=_params(("arbitrary", "arbitrary")),
        name="mlp",
    )(h, g_mlp, w_up, w_down, g_final)


def kernel(x, norm_mix_g, w_in, w_fourier, w_out, norm_mlp_g, w_up, w_down, norm_final_g):
    batch, seq, d_model = x.shape
    depth = w_in.shape[0]
    fourier_width = w_fourier.shape[1] * w_fourier.shape[2]
    n_heads = (w_in.shape[2] - fourier_width) // (3 * HEAD_DIM)
    assert w_in.shape[2] == 4 * fourier_width and n_heads * HEAD_DIM == fourier_width
    assert seq % (max(DILATIONS) * K_WINDOW) == 0
    assert depth == 1, "the MLP kernel fuses the final RMSNorm, so it handles a single layer"

    h = x.reshape(batch * seq, d_model)
    heads_by_dilation = _in_proj(h, norm_mix_g[0][None, :], w_in[0].astype(BF16),
                                 batch, seq, block_m=1024)
    yf = _fourier_mix(heads_by_dilation[DILATIONS.index(DFT_RADIX)], w_fourier[0].astype(BF16))
    ya = _attention(heads_by_dilation, n_heads)
    h = _out_proj(yf.reshape(batch * seq, fourier_width),
                  ya.reshape(batch * seq, n_heads * HEAD_DIM),
                  h, w_out[0].astype(BF16), block_m=512)
    h = _mlp(h, norm_mlp_g[0][None, :], w_up[0].astype(BF16), w_down[0].astype(BF16),
             norm_final_g[None, :], block_m=1024, block_f=512, row_chunk=512)
    return h.reshape(batch, seq, d_model)
```

```python
import functools

import numpy as np
import jax
import jax.numpy as jnp
from jax import lax
from jax.experimental import pallas as pl
from jax.experimental.pallas import tpu as pltpu

F32 = jnp.float32
BF16 = jnp.bfloat16

HEAD_DIM = 128
WINDOW_DILATIONS = ((128, 1), (512, 4), (2048, 16))
DILATIONS = tuple(d for _, d in WINDOW_DILATIONS)
RADIUS = 64
assert all((w // 2) // d == RADIUS for w, d in WINDOW_DILATIONS)
RMS_EPS = 1e-6
MASK_VALUE = -1e30
LOG2_E = 1.4426950408889634

Q_BLOCK = 128
K_WINDOW = Q_BLOCK + 2 * RADIUS
DFT_RADIX = 4
MLP_BLOCK_F = 512

VMEM_LIMIT_BYTES = 60 * 1024 * 1024


def _params(semantics):
    return pltpu.CompilerParams(dimension_semantics=semantics, vmem_limit_bytes=VMEM_LIMIT_BYTES)


def _resident(block_shape, index_map):
    return pl.BlockSpec(block_shape, index_map, pipeline_mode=pl.Buffered(1))


def _rmsnorm_f32(x, g):
    inv = lax.rsqrt(jnp.mean(x * x, axis=-1, keepdims=True) + RMS_EPS)
    return x * inv * g


def _in_proj_kernel(x_ref, g_ref, w_ref, nat_ref, c4_ref, c16_ref, u_scr, nat_scr, c4_scr, *,
                    heads_per_block, q_scale):
    j = pl.program_id(1)
    block_m = u_scr.shape[0]
    chunk_heads = 2
    chunk_n = chunk_heads * HEAD_DIM

    @pl.when(j == 0)
    def _():
        u_scr[...] = _rmsnorm_f32(x_ref[...], g_ref[...]).astype(BF16)

    scale = jnp.where(j == 1, q_scale, 1.0)
    for c in range(heads_per_block // chunk_heads):
        acc = jnp.dot(u_scr[...], w_ref[:, c * chunk_n:(c + 1) * chunk_n],
                      preferred_element_type=F32) * scale
        for hh in range(c * chunk_heads, (c + 1) * chunk_heads):
            head = acc[:, (hh % chunk_heads) * HEAD_DIM:(hh % chunk_heads + 1) * HEAD_DIM]
            nat_ref[0, hh, 0] = head.astype(BF16)
            nat_scr[hh] = head
            for r in range(4):
                piece = nat_scr[hh, pl.ds(r, block_m // 4, stride=4), :]
                c4_scr[hh, r] = piece
                c4_ref[0, hh, r] = piece.astype(BF16)
            for r in range(4):
                for a in range(4):
                    piece = c4_scr[hh, r, pl.ds(a, block_m // 16, stride=4), :]
                    c16_ref[0, hh, 4 * a + r] = piece.astype(BF16)


def _in_proj(x2, g, w_in, batch, seq, block_m):
    n_tok, d_model = x2.shape
    n_col_blocks, _, block_n = w_in.shape
    in_width = n_col_blocks * block_n
    heads_per_block = block_n // HEAD_DIM
    tiles_per_batch = seq // block_m
    kernel = functools.partial(_in_proj_kernel, heads_per_block=heads_per_block,
                               q_scale=HEAD_DIM ** -0.5 * LOG2_E)

    def head_block(dil):
        return pl.BlockSpec((1, heads_per_block, dil, block_m // dil, HEAD_DIM),
                            lambda i, j: (i // tiles_per_batch, j, 0, i % tiles_per_batch, 0))

    def head_shape(dil):
        return jax.ShapeDtypeStruct((batch, in_width // HEAD_DIM, dil, seq // dil, HEAD_DIM), BF16)

    return pl.pallas_call(
        kernel,
        grid=(n_tok // block_m, n_col_blocks),
        in_specs=[
            pl.BlockSpec((block_m, d_model), lambda i, j: (i, 0)),
            _resident((1, d_model), lambda i, j: (0, 0)),
            pl.BlockSpec((None, d_model, block_n), lambda i, j: (j, 0, 0)),
        ],
        out_specs=[head_block(dil) for dil in DILATIONS],
        out_shape=[head_shape(dil) for dil in DILATIONS],
        scratch_shapes=[pltpu.VMEM((block_m, d_model), BF16),
                        pltpu.VMEM((heads_per_block, block_m, HEAD_DIM), F32),
                        pltpu.VMEM((heads_per_block, 4, block_m // 4, HEAD_DIM), F32)],
        compiler_params=_params(("arbitrary", "arbitrary")),
        name="in_proj",
    )(x2, g, w_in)


def _dft_tables(seq, group_dim):
    quarter = seq // DFT_RADIX
    s = np.arange(quarter, dtype=np.int64)[:, None]
    tabs = []
    for r in range(DFT_RADIX):
        t = DFT_RADIX * np.arange(quarter, dtype=np.int64)[None, :] + r
        ang = 2.0 * np.pi * ((s * t) % seq).astype(np.float64) / seq
        tabs.append(np.concatenate([np.cos(ang), -np.sin(ang)], axis=0))
    seq_tab = np.stack(tabs).astype(np.float32)
    c = np.arange(group_dim, dtype=np.int64)
    ang = 2.0 * np.pi * ((c[:, None] * c[None, :]) % group_dim).astype(np.float64) / group_dim
    ortho = 1.0 / np.sqrt(float(seq) * float(group_dim))
    chan_tab = (np.concatenate([np.cos(ang), np.sin(ang)], axis=0) * ortho).astype(np.float32)
    return seq_tab, chan_tab


def _fourier_kernel(tab_ref, u_ref, chan_ref, wf_ref, y_ref):
    slabs, _, quarter, _ = u_ref.shape
    re, im = [], []
    for r in range(DFT_RADIX):
        u = jnp.concatenate([u_ref[i, r] for i in range(slabs)], axis=1)
        g = jnp.dot(tab_ref[r], u, preferred_element_type=F32)
        re.append(g[:quarter])
        im.append(g[quarter:])
    z = (
        (re[0] + re[1] + re[2] + re[3], im[0] + im[1] + im[2] + im[3]),
        (re[0] + im[1] - re[2] - im[3], im[0] - re[1] - im[2] + re[3]),
        (re[0] - re[1] + re[2] - re[3], im[0] - im[1] + im[2] - im[3]),
        (re[0] - im[1] - re[2] + im[3], im[0] + re[1] - im[2] - re[3]),
    )
    for q, (zr, zi) in enumerate(z):
        zc = jnp.concatenate([zr, zi], axis=1).astype(BF16)
        real = jnp.dot(zc, chan_ref[...], preferred_element_type=F32).astype(BF16)
        y = jnp.dot(real, wf_ref[0], preferred_element_type=F32)
        y_ref[0, q * quarter:(q + 1) * quarter, :] = y.astype(y_ref.dtype)


def _fourier_mix(heads4, w_fourier):
    batch, _, radix, quarter, _ = heads4.shape
    assert radix == DFT_RADIX
    seq = radix * quarter
    n_groups, group_dim, _ = w_fourier.shape
    fw = n_groups * group_dim
    slabs = group_dim // HEAD_DIM
    seq_tab, chan_tab = _dft_tables(seq, group_dim)
    seq_tab = jnp.asarray(seq_tab).astype(BF16)
    chan_tab = jnp.asarray(chan_tab).astype(BF16)

    return pl.pallas_call(
        _fourier_kernel,
        grid=(batch, n_groups),
        in_specs=[
            _resident((DFT_RADIX, 2 * quarter, quarter), lambda b, g: (0, 0, 0)),
            pl.BlockSpec((None, slabs, DFT_RADIX, quarter, HEAD_DIM), lambda b, g: (b, g, 0, 0, 0)),
            _resident((2 * group_dim, group_dim), lambda b, g: (0, 0)),
            pl.BlockSpec((1, group_dim, group_dim), lambda b, g: (g, 0, 0)),
        ],
        out_specs=pl.BlockSpec((1, seq, group_dim), lambda b, g: (b, 0, g)),
        out_shape=jax.ShapeDtypeStruct((batch, seq, fw), BF16),
        compiler_params=_params(("arbitrary", "arbitrary")),
        name="fourier_mix",
    )(seq_tab, heads4, chan_tab, w_fourier)


SCRATCH_PITCH = {1: 1, 4: 4, 16: 24}


def _attention_kernel(slopes_ref,
                      q1_ref, k1_ref, v1_ref, q4_ref, k4_ref, v4_ref, q16_ref, k16_ref, v16_ref,
                      w0_ref, w1_ref, w2_ref,
                      o_ref, w0_out, w1_out, w2_out,
                      bias_scr, acc4_scr, m4_scr, den4_scr, acc16_scr, m16_scr, den16_scr):
    seq = o_ref.shape[0]
    slope = slopes_ref[pl.program_id(1)]

    for src, dst in ((w0_ref, w0_out), (w1_ref, w1_out), (w2_ref, w2_out)):
        tiles, _, width = dst.shape
        for t in range(tiles):
            dst[t] = src[:, t * width:(t + 1) * width].astype(dst.dtype)

    row = lax.broadcasted_iota(jnp.int32, (Q_BLOCK, K_WINDOW), 0)
    col = lax.broadcasted_iota(jnp.int32, (Q_BLOCK, K_WINDOW), 1)
    for gi, dil in enumerate(DILATIONS):
        for oi in range(3):
            rel = jnp.abs(col - row - oi * RADIUS)
            penalty = (slope * LOG2_E) * (dil * rel).astype(F32)
            bias_scr[gi * 3 + oi] = jnp.where(rel <= RADIUS, -penalty, MASK_VALUE)

    ones = jnp.ones((K_WINDOW, HEAD_DIM), BF16)
    branches = (
        (q1_ref, k1_ref, v1_ref, None, None, None),
        (q4_ref, k4_ref, v4_ref, acc4_scr, m4_scr, den4_scr),
        (q16_ref, k16_ref, v16_ref, acc16_scr, m16_scr, den16_scr),
    )

    def natural_rows(scr, gi, qs):
        dil, pitch = DILATIONS[gi], SCRATCH_PITCH[DILATIONS[gi]]
        if pitch == dil:
            return scr[qs:qs + Q_BLOCK, :]
        base = qs // dil * pitch
        return jnp.concatenate([scr[base + i * pitch:base + i * pitch + dil, :]
                                for i in range(Q_BLOCK // dil)], axis=0)

    def block(gi, cls, qs):
        q_ref, k_ref, v_ref, acc_scr, m_scr, den_scr = branches[gi]
        members = seq // DILATIONS[gi]
        ws = min(max(qs - RADIUS, 0), members - K_WINDOW)
        q = q_ref[cls, qs:qs + Q_BLOCK, :]
        k = k_ref[cls, ws:ws + K_WINDOW, :]
        v = v_ref[cls, ws:ws + K_WINDOW, :]
        s = lax.dot_general(q, k, (((1,), (1,)), ((), ())), preferred_element_type=F32)
        s = s + bias_scr[gi * 3 + (qs - ws) // RADIUS]
        m = jnp.max(s, axis=1, keepdims=True)
        p = jnp.exp2(s - m).astype(BF16)
        acc = jnp.dot(p, jnp.concatenate([v, ones], axis=1), preferred_element_type=F32)
        num, den = acc[:, :HEAD_DIM], acc[:, HEAD_DIM:]
        m = jnp.broadcast_to(m, (Q_BLOCK, HEAD_DIM))
        if gi > 0:
            pitch = SCRATCH_PITCH[DILATIONS[gi]]
            rows = pl.ds(qs * pitch + cls, Q_BLOCK, stride=pitch)
            acc_scr[rows, :] = num
            m_scr[rows, :] = m
            den_scr[rows, :] = den
            return
        m4, m16 = natural_rows(m4_scr, 1, qs), natural_rows(m16_scr, 2, qs)
        top = jnp.maximum(m, jnp.maximum(m4, m16))
        e1, e4, e16 = jnp.exp2(m - top), jnp.exp2(m4 - top), jnp.exp2(m16 - top)
        num = e1 * num + e4 * natural_rows(acc4_scr, 1, qs) + e16 * natural_rows(acc16_scr, 2, qs)
        den = e1 * den + e4 * natural_rows(den4_scr, 1, qs) + e16 * natural_rows(den16_scr, 2, qs)
        o_ref[qs:qs + Q_BLOCK, :] = (num * pl.reciprocal(den, approx=True)).astype(o_ref.dtype)

    for gi in (2, 1, 0):
        members = seq // DILATIONS[gi]
        for cls in range(DILATIONS[gi]):
            for qs in range(0, members, Q_BLOCK):
                block(gi, cls, qs)


def _attention(heads_by_dilation, n_heads, weights):
    batch, n_all, _, seq, _ = heads_by_dilation[0].shape
    first = n_all - 3 * n_heads
    steps = batch * n_heads
    w_operands, w_in_specs, w_out_specs, w_out_shapes = [], [], [], []
    for w, width in weights:
        rows, cols = w.shape
        w_operands.append(w.reshape(steps, rows // steps, cols))
        w_in_specs.append(pl.BlockSpec((None, rows // steps, cols),
                                       lambda b, h: (b * n_heads + h, 0, 0)))
        w_out_specs.append(pl.BlockSpec((cols // width, rows // steps, width),
                                        lambda b, h: (0, b * n_heads + h, 0)))
        w_out_shapes.append(jax.ShapeDtypeStruct((cols // width, rows, width), BF16))
    slopes = jnp.asarray(2.0 ** (-8.0 * (np.arange(n_heads) + 1) / n_heads), dtype=F32)
    operands, in_specs = [], []
    for dil, heads in zip(DILATIONS, heads_by_dilation):
        for which in range(3):
            operands.append(heads)
            in_specs.append(pl.BlockSpec((None, None, dil, seq // dil, HEAD_DIM),
                                         lambda b, h, which=which:
                                         (b, first + which * n_heads + h, 0, 0, 0)))
    result = lambda dil: pltpu.VMEM((seq // dil * SCRATCH_PITCH[dil], HEAD_DIM), F32)
    return pl.pallas_call(
        _attention_kernel,
        grid=(batch, n_heads),
        in_specs=[pl.BlockSpec(memory_space=pltpu.SMEM)] + in_specs + w_in_specs,
        out_specs=[pl.BlockSpec((None, seq, HEAD_DIM), lambda b, h: (b, 0, h))] + w_out_specs,
        out_shape=[jax.ShapeDtypeStruct((batch, seq, n_heads * HEAD_DIM), BF16)] + w_out_shapes,
        scratch_shapes=[pltpu.VMEM((3 * len(DILATIONS), Q_BLOCK, K_WINDOW), F32),
                        result(4), result(4), result(4), result(16), result(16), result(16)],
        compiler_params=_params(("arbitrary", "arbitrary")),
        name="dilated_attention",
    )(slopes, *operands, *w_operands)


def _out_proj_kernel(yf_ref, ya_ref, x_ref, w_ref, h_ref):
    fw = yf_ref.shape[1]
    acc = jnp.dot(yf_ref[...], w_ref[:fw, :], preferred_element_type=F32)
    acc = acc + jnp.dot(ya_ref[...], w_ref[fw:, :], preferred_element_type=F32)
    h_ref[...] = x_ref[...] + acc


def _out_proj(yf, ya, x2, w_out, block_m):
    n_tok, d_model = x2.shape
    fw, aw = yf.shape[1], ya.shape[1]
    return pl.pallas_call(
        _out_proj_kernel,
        grid=(n_tok // block_m,),
        in_specs=[
            pl.BlockSpec((block_m, fw), lambda i: (i, 0)),
            pl.BlockSpec((block_m, aw), lambda i: (i, 0)),
            pl.BlockSpec((block_m, d_model), lambda i: (i, 0)),
            _resident((fw + aw, d_model), lambda i: (0, 0)),
        ],
        out_specs=pl.BlockSpec((block_m, d_model), lambda i: (i, 0)),
        out_shape=jax.ShapeDtypeStruct((n_tok, d_model), F32),
        compiler_params=_params(("arbitrary",)),
        name="out_proj",
    )(yf, ya, x2, w_out)


def _mlp_kernel(h_ref, g_ref, wu_ref, wd_ref, gf_ref, o_ref, u_scr, *, row_chunk):
    f = pl.program_id(1)
    last = pl.num_programs(1) - 1

    @pl.when(f == 0)
    def _():
        u_scr[...] = _rmsnorm_f32(h_ref[...], g_ref[...]).astype(BF16)

    def contribution(rows):
        a = jnp.dot(u_scr[rows, :], wu_ref[...], preferred_element_type=F32)
        a = jnp.square(jnp.maximum(a, 0.0)).astype(BF16)
        return jnp.dot(a, wd_ref[...], preferred_element_type=F32)

    chunks = [slice(c, c + row_chunk) for c in range(0, o_ref.shape[0], row_chunk)]

    @pl.when(f == 0)
    def _():
        for rows in chunks:
            o_ref[rows, :] = h_ref[rows, :] + contribution(rows)

    @pl.when(jnp.logical_and(f > 0, f < last))
    def _():
        for rows in chunks:
            o_ref[rows, :] += contribution(rows)

    @pl.when(f == last)
    def _():
        for rows in chunks:
            o_ref[rows, :] = _rmsnorm_f32(o_ref[rows, :] + contribution(rows), gf_ref[...])


def _mlp(h, g_mlp, w_up_tiles, w_down, g_final, block_m, row_chunk):
    n_tok, d_model = h.shape
    f_tiles, _, block_f = w_up_tiles.shape
    kernel = functools.partial(_mlp_kernel, row_chunk=row_chunk)
    return pl.pallas_call(
        kernel,
        grid=(n_tok // block_m, f_tiles),
        in_specs=[
            pl.BlockSpec((block_m, d_model), lambda i, f: (i, 0)),
            _resident((1, d_model), lambda i, f: (0, 0)),
            pl.BlockSpec((None, d_model, block_f), lambda i, f: (f, 0, 0)),
            pl.BlockSpec((block_f, d_model), lambda i, f: (f, 0)),
            _resident((1, d_model), lambda i, f: (0, 0)),
        ],
        out_specs=pl.BlockSpec((block_m, d_model), lambda i, f: (i, 0)),
        out_shape=jax.ShapeDtypeStruct((n_tok, d_model), F32),
        scratch_shapes=[pltpu.VMEM((block_m, d_model), BF16)],
        compiler_params=_params(("arbitrary", "arbitrary")),
        name="mlp",
    )(h, g_mlp, w_up_tiles, w_down, g_final)


def kernel(x, norm_mix_g, w_in, w_fourier, w_out, norm_mlp_g, w_up, w_down, norm_final_g):
    batch, seq, d_model = x.shape
    depth = w_in.shape[0]
    fourier_width = w_fourier.shape[1] * w_fourier.shape[2]
    n_heads = (w_in.shape[2] - fourier_width) // (3 * HEAD_DIM)
    assert w_in.shape[2] == 4 * fourier_width and n_heads * HEAD_DIM == fourier_width
    assert seq % (max(DILATIONS) * K_WINDOW) == 0
    assert depth == 1, "the MLP kernel fuses the final RMSNorm, so it handles a single layer"

    h = x.reshape(batch * seq, d_model)
    w_in_tiles = w_in[0].astype(BF16).reshape(d_model, 4, -1).transpose(1, 0, 2)
    heads_by_dilation = _in_proj(h, norm_mix_g[0][None, :], w_in_tiles, batch, seq, block_m=1024)
    yf = _fourier_mix(heads_by_dilation[DILATIONS.index(DFT_RADIX)], w_fourier[0].astype(BF16))
    d_ff = w_up.shape[2]
    ya, w_up_tiles, w_down_b, w_out_b = _attention(
        heads_by_dilation, n_heads,
        weights=((w_up[0], MLP_BLOCK_F), (w_down[0], d_model), (w_out[0], d_model)))
    h = _out_proj(yf.reshape(batch * seq, fourier_width),
                  ya.reshape(batch * seq, n_heads * HEAD_DIM),
                  h, w_out_b.reshape(d_model, d_model), block_m=512)
    h = _mlp(h, norm_mlp_g[0][None, :], w_up_tiles, w_down_b.reshape(d_ff, d_model),
             norm_final_g[None, :], block_m=1024, row_chunk=512)
    return h.reshape(batch, seq, d_model)
```

```python
import functools

import numpy as np
import jax
import jax.numpy as jnp
from jax import lax
from jax.experimental import pallas as pl
from jax.experimental.pallas import tpu as pltpu

F32 = jnp.float32
BF16 = jnp.bfloat16

HEAD_DIM = 128
WINDOW_DILATIONS = ((128, 1), (512, 4), (2048, 16))
DILATIONS = tuple(d for _, d in WINDOW_DILATIONS)
RADIUS = 64
assert all((w // 2) // d == RADIUS for w, d in WINDOW_DILATIONS)
RMS_EPS = 1e-6
MASK_VALUE = -1e30
LOG2_E = 1.4426950408889634

CLASSES = max(DILATIONS)
Q_BLOCK = 128
K_WINDOW = Q_BLOCK + 2 * RADIUS
DFT_RADIX = 4
MLP_BLOCK_F = 512
RESULT_PITCH = 24

VMEM_LIMIT_BYTES = 60 * 1024 * 1024


def _params(semantics):
    return pltpu.CompilerParams(dimension_semantics=semantics, vmem_limit_bytes=VMEM_LIMIT_BYTES)


def _resident(block_shape, index_map):
    return pl.BlockSpec(block_shape, index_map, pipeline_mode=pl.Buffered(1))


def _rmsnorm_f32(x, g):
    inv = lax.rsqrt(jnp.mean(x * x, axis=-1, keepdims=True) + RMS_EPS)
    return x * inv * g


def _in_proj_kernel(x_ref, g_ref, w_ref, nat_ref, cls_ref, u_scr, nat_scr, c4_scr, *,
                    heads_per_block, q_scale):
    j = pl.program_id(1)
    block_m = u_scr.shape[0]
    chunk_heads = 2
    chunk_n = chunk_heads * HEAD_DIM

    @pl.when(j == 0)
    def _():
        u_scr[...] = _rmsnorm_f32(x_ref[...], g_ref[...]).astype(BF16)

    scale = jnp.where(j == 1, q_scale, 1.0)
    for c in range(heads_per_block // chunk_heads):
        acc = jnp.dot(u_scr[...], w_ref[:, c * chunk_n:(c + 1) * chunk_n],
                      preferred_element_type=F32) * scale
        for hh in range(c * chunk_heads, (c + 1) * chunk_heads):
            head = acc[:, (hh % chunk_heads) * HEAD_DIM:(hh % chunk_heads + 1) * HEAD_DIM]
            nat_ref[0, hh, 0] = head.astype(BF16)
            nat_scr[hh] = head
            for r in range(4):
                c4_scr[hh, r] = nat_scr[hh, pl.ds(r, block_m // 4, stride=4), :]
            for r in range(4):
                for a in range(4):
                    piece = c4_scr[hh, r, pl.ds(a, block_m // 16, stride=4), :]
                    cls_ref[0, hh, 4 * a + r] = piece.astype(BF16)


def _in_proj(x2, g, w_in, batch, seq, block_m):
    n_tok, d_model = x2.shape
    in_width = w_in.shape[1]
    block_n = in_width // 4
    heads_per_block = block_n // HEAD_DIM
    tiles_per_batch = seq // block_m
    kernel = functools.partial(_in_proj_kernel, heads_per_block=heads_per_block,
                               q_scale=HEAD_DIM ** -0.5 * LOG2_E)

    def head_block(classes, first_block):
        return pl.BlockSpec((1, heads_per_block, classes, block_m // classes, HEAD_DIM),
                            lambda i, j: (i // tiles_per_batch, jnp.maximum(j - first_block, 0), 0,
                                          i % tiles_per_batch, 0))

    def head_shape(classes, first_block):
        heads = in_width // HEAD_DIM - first_block * heads_per_block
        return jax.ShapeDtypeStruct((batch, heads, classes, seq // classes, HEAD_DIM), BF16)

    return pl.pallas_call(
        kernel,
        grid=(n_tok // block_m, in_width // block_n),
        in_specs=[
            pl.BlockSpec((block_m, d_model), lambda i, j: (i, 0)),
            _resident((1, d_model), lambda i, j: (0, 0)),
            pl.BlockSpec((d_model, block_n), lambda i, j: (0, j)),
        ],
        out_specs=[head_block(1, 1), head_block(CLASSES, 0)],
        out_shape=[head_shape(1, 1), head_shape(CLASSES, 0)],
        scratch_shapes=[pltpu.VMEM((block_m, d_model), BF16),
                        pltpu.VMEM((heads_per_block, block_m, HEAD_DIM), F32),
                        pltpu.VMEM((heads_per_block, 4, block_m // 4, HEAD_DIM), F32)],
        compiler_params=_params(("arbitrary", "arbitrary")),
        name="in_proj",
    )(x2, g, w_in)


def _dft_tables(seq, group_dim):
    quarter = seq // DFT_RADIX
    sub = CLASSES // DFT_RADIX
    s = np.arange(quarter, dtype=np.int64)[:, None]
    t_prime = np.array([sub * m + a for a in range(sub) for m in range(seq // CLASSES)],
                       dtype=np.int64)[None, :]
    tabs = []
    for r in range(DFT_RADIX):
        ang = 2.0 * np.pi * ((s * (DFT_RADIX * t_prime + r)) % seq).astype(np.float64) / seq
        tabs.append(np.concatenate([np.cos(ang), -np.sin(ang)], axis=0))
    seq_tab = np.stack(tabs).astype(np.float32)
    c = np.arange(group_dim, dtype=np.int64)
    ang = 2.0 * np.pi * ((c[:, None] * c[None, :]) % group_dim).astype(np.float64) / group_dim
    ortho = 1.0 / np.sqrt(float(seq) * float(group_dim))
    chan_tab = (np.concatenate([np.cos(ang), np.sin(ang)], axis=0) * ortho).astype(np.float32)
    return seq_tab, chan_tab


def _fourier_kernel(tab_ref, u_ref, chan_ref, wf_ref, y_ref):
    slabs, classes, members, _ = u_ref.shape
    quarter = classes * members // DFT_RADIX
    re, im = [], []
    for r in range(DFT_RADIX):
        u = jnp.concatenate(
            [jnp.concatenate([u_ref[i, DFT_RADIX * a + r] for i in range(slabs)], axis=1)
             for a in range(classes // DFT_RADIX)], axis=0)
        g = jnp.dot(tab_ref[r], u, preferred_element_type=F32)
        re.append(g[:quarter])
        im.append(g[quarter:])
    er, ei, fr, fi = re[0] + re[2], im[0] + im[2], re[0] - re[2], im[0] - im[2]
    gr, gi, hr, hi = re[1] + re[3], im[1] + im[3], re[1] - re[3], im[1] - im[3]
    z = ((er + gr, ei + gi), (fr + hi, fi - hr), (er - gr, ei - gi), (fr - hi, fi + hr))
    for q, (zr, zi) in enumerate(z):
        zc = jnp.concatenate([zr, zi], axis=1).astype(BF16)
        real = jnp.dot(zc, chan_ref[...], preferred_element_type=F32).astype(BF16)
        y = jnp.dot(real, wf_ref[0], preferred_element_type=F32)
        y_ref[0, q * quarter:(q + 1) * quarter, :] = y.astype(y_ref.dtype)


def _fourier_mix(heads_cls, w_fourier):
    batch, _, classes, members, _ = heads_cls.shape
    seq = classes * members
    quarter = seq // DFT_RADIX
    n_groups, group_dim, _ = w_fourier.shape
    fw = n_groups * group_dim
    slabs = group_dim // HEAD_DIM
    seq_tab, chan_tab = _dft_tables(seq, group_dim)
    seq_tab = jnp.asarray(seq_tab).astype(BF16)
    chan_tab = jnp.asarray(chan_tab).astype(BF16)

    return pl.pallas_call(
        _fourier_kernel,
        grid=(batch, n_groups),
        in_specs=[
            _resident((DFT_RADIX, 2 * quarter, quarter), lambda b, g: (0, 0, 0)),
            pl.BlockSpec((None, slabs, classes, members, HEAD_DIM), lambda b, g: (b, g, 0, 0, 0)),
            _resident((2 * group_dim, group_dim), lambda b, g: (0, 0)),
            pl.BlockSpec((1, group_dim, group_dim), lambda b, g: (g, 0, 0)),
        ],
        out_specs=pl.BlockSpec((1, seq, group_dim), lambda b, g: (b, 0, g)),
        out_shape=jax.ShapeDtypeStruct((batch, seq, fw), BF16),
        compiler_params=_params(("arbitrary", "arbitrary")),
        name="fourier_mix",
    )(seq_tab, heads_cls, chan_tab, w_fourier)


def _attention_blocks(dil, seq):
    sub = CLASSES // dil
    q_rows, k_rows, radius = Q_BLOCK // sub, K_WINDOW // sub, RADIUS // sub
    members = seq // CLASSES
    for r in range(dil):
        for m0 in range(0, members, q_rows):
            w0 = min(max(m0 - radius, 0), members - k_rows)
            yield ((m0 - w0) // radius,
                   [(dil * a + r, m0, q_rows) for a in range(sub)],
                   [(dil * a + r, w0, k_rows) for a in range(sub)],
                   [(a * q_rows, q_rows, RESULT_PITCH * m0 + dil * a + r) for a in range(sub)])


def _attention_kernel(slopes_ref, q1_ref, k1_ref, v1_ref, qc_ref, kc_ref, vc_ref,
                      w0_ref, w1_ref, w2_ref,
                      o_ref, w0_out, w1_out, w2_out,
                      bias_scr, acc4_scr, m4_scr, den4_scr, acc16_scr, m16_scr, den16_scr):
    seq = o_ref.shape[0]
    slope = slopes_ref[pl.program_id(1)]

    for src, dst in ((w0_ref, w0_out), (w1_ref, w1_out), (w2_ref, w2_out)):
        tiles, _, width = dst.shape
        for t in range(tiles):
            dst[t] = src[:, t * width:(t + 1) * width].astype(dst.dtype)

    row = lax.broadcasted_iota(jnp.int32, (Q_BLOCK, K_WINDOW), 0)
    col = lax.broadcasted_iota(jnp.int32, (Q_BLOCK, K_WINDOW), 1)
    for gi, dil in enumerate(DILATIONS):
        sub = CLASSES // dil if dil > 1 else 1
        q_rows, k_rows, radius = Q_BLOCK // sub, K_WINDOW // sub, RADIUS // sub
        for variant in range(3):
            rel = sub * (col % k_rows - row % q_rows - variant * radius) + (col // k_rows - row // q_rows)
            rel = jnp.abs(rel)
            penalty = (slope * LOG2_E) * (dil * rel).astype(F32)
            bias_scr[gi * 3 + variant] = jnp.where(rel <= RADIUS, -penalty, MASK_VALUE)

    ones = jnp.ones((K_WINDOW, HEAD_DIM), BF16)

    def gather(ref, pieces):
        parts = [ref[c, r0:r0 + n, :] for c, r0, n in pieces]
        return parts[0] if len(parts) == 1 else jnp.concatenate(parts, axis=0)

    def softmax_block(gi, variant, q, k, v):
        s = lax.dot_general(q, k, (((1,), (1,)), ((), ())), preferred_element_type=F32)
        s = s + bias_scr[gi * 3 + variant]
        m = jnp.max(s, axis=1, keepdims=True)
        p = jnp.exp2(s - m).astype(BF16)
        acc = jnp.dot(p, jnp.concatenate([v, ones], axis=1), preferred_element_type=F32)
        return acc[:, :HEAD_DIM], jnp.broadcast_to(m, (Q_BLOCK, HEAD_DIM)), acc[:, HEAD_DIM:]

    def natural_rows(scr, qs):
        base = qs // CLASSES * RESULT_PITCH
        return jnp.concatenate([scr[base + i * RESULT_PITCH:base + i * RESULT_PITCH + CLASSES, :]
                                for i in range(Q_BLOCK // CLASSES)], axis=0)

    for gi, scratch in ((2, (acc16_scr, m16_scr, den16_scr)), (1, (acc4_scr, m4_scr, den4_scr))):
        for variant, q_pieces, k_pieces, out_pieces in _attention_blocks(DILATIONS[gi], seq):
            results = softmax_block(gi, variant, gather(qc_ref, q_pieces), gather(kc_ref, k_pieces),
                                    gather(vc_ref, k_pieces))
            for first, rows, start in out_pieces:
                for scr, value in zip(scratch, results):
                    scr[pl.ds(start, rows, stride=RESULT_PITCH), :] = value[first:first + rows]

    for qs in range(0, seq, Q_BLOCK):
        ws = min(max(qs - RADIUS, 0), seq - K_WINDOW)
        num, m, den = softmax_block(0, (qs - ws) // RADIUS, q1_ref[0, qs:qs + Q_BLOCK, :],
                                    k1_ref[0, ws:ws + K_WINDOW, :], v1_ref[0, ws:ws + K_WINDOW, :])
        m4, m16 = natural_rows(m4_scr, qs), natural_rows(m16_scr, qs)
        top = jnp.maximum(m, jnp.maximum(m4, m16))
        e1, e4, e16 = jnp.exp2(m - top), jnp.exp2(m4 - top), jnp.exp2(m16 - top)
        num = e1 * num + e4 * natural_rows(acc4_scr, qs) + e16 * natural_rows(acc16_scr, qs)
        den = e1 * den + e4 * natural_rows(den4_scr, qs) + e16 * natural_rows(den16_scr, qs)
        o_ref[qs:qs + Q_BLOCK, :] = (num * pl.reciprocal(den, approx=True)).astype(o_ref.dtype)


def _attention(heads_nat, heads_cls, n_heads, weights):
    batch, _, _, seq, _ = heads_nat.shape
    operands, in_specs = [], []
    for heads in (heads_nat, heads_cls):
        first = heads.shape[1] - 3 * n_heads
        for which in range(3):
            operands.append(heads)
            in_specs.append(pl.BlockSpec((None, None) + heads.shape[2:],
                                         lambda b, h, which=which, first=first:
                                         (b, first + which * n_heads + h, 0, 0, 0)))
    steps = batch * n_heads
    w_operands, w_in_specs, w_out_specs, w_out_shapes = [], [], [], []
    for w, width in weights:
        rows, cols = w.shape
        w_operands.append(w.reshape(steps, rows // steps, cols))
        w_in_specs.append(pl.BlockSpec((None, rows // steps, cols),
                                       lambda b, h: (b * n_heads + h, 0, 0)))
        w_out_specs.append(pl.BlockSpec((cols // width, rows // steps, width),
                                        lambda b, h: (0, b * n_heads + h, 0)))
        w_out_shapes.append(jax.ShapeDtypeStruct((cols // width, rows, width), BF16))
    slopes = jnp.asarray(2.0 ** (-8.0 * (np.arange(n_heads) + 1) / n_heads), dtype=F32)
    result = lambda: pltpu.VMEM((seq // CLASSES * RESULT_PITCH, HEAD_DIM), F32)
    return pl.pallas_call(
        _attention_kernel,
        grid=(batch, n_heads),
        in_specs=[pl.BlockSpec(memory_space=pltpu.SMEM)] + in_specs + w_in_specs,
        out_specs=[pl.BlockSpec((None, seq, HEAD_DIM), lambda b, h: (b, 0, h))] + w_out_specs,
        out_shape=[jax.ShapeDtypeStruct((batch, seq, n_heads * HEAD_DIM), BF16)] + w_out_shapes,
        scratch_shapes=[pltpu.VMEM((3 * len(DILATIONS), Q_BLOCK, K_WINDOW), F32)]
        + [result() for _ in range(6)],
        compiler_params=_params(("arbitrary", "arbitrary")),
        name="dilated_attention",
    )(slopes, *operands, *w_operands)


def _out_proj_kernel(yf_ref, ya_ref, x_ref, w_ref, h_ref):
    fw = yf_ref.shape[1]
    acc = jnp.dot(yf_ref[...], w_ref[:fw, :], preferred_element_type=F32)
    acc = acc + jnp.dot(ya_ref[...], w_ref[fw:, :], preferred_element_type=F32)
    h_ref[...] = x_ref[...] + acc


def _out_proj(yf, ya, x2, w_out, block_m):
    n_tok, d_model = x2.shape
    fw, aw = yf.shape[1], ya.shape[1]
    return pl.pallas_call(
        _out_proj_kernel,
        grid=(n_tok // block_m,),
        in_specs=[
            pl.BlockSpec((block_m, fw), lambda i: (i, 0)),
            pl.BlockSpec((block_m, aw), lambda i: (i, 0)),
            pl.BlockSpec((block_m, d_model), lambda i: (i, 0)),
            _resident((fw + aw, d_model), lambda i: (0, 0)),
        ],
        out_specs=pl.BlockSpec((block_m, d_model), lambda i: (i, 0)),
        out_shape=jax.ShapeDtypeStruct((n_tok, d_model), F32),
        compiler_params=_params(("arbitrary",)),
        name="out_proj",
    )(yf, ya, x2, w_out)


def _mlp_kernel(h_ref, g_ref, wu_ref, wd_ref, gf_ref, o_ref, u_scr, *, row_chunk):
    f = pl.program_id(1)
    last = pl.num_programs(1) - 1

    @pl.when(f == 0)
    def _():
        u_scr[...] = _rmsnorm_f32(h_ref[...], g_ref[...]).astype(BF16)

    def contribution(rows):
        a = jnp.dot(u_scr[rows, :], wu_ref[...], preferred_element_type=F32)
        a = jnp.square(jnp.maximum(a, 0.0)).astype(BF16)
        return jnp.dot(a, wd_ref[...], preferred_element_type=F32)

    chunks = [slice(c, c + row_chunk) for c in range(0, o_ref.shape[0], row_chunk)]

    @pl.when(f == 0)
    def _():
        for rows in chunks:
            o_ref[rows, :] = h_ref[rows, :] + contribution(rows)

    @pl.when(jnp.logical_and(f > 0, f < last))
    def _():
        for rows in chunks:
            o_ref[rows, :] += contribution(rows)

    @pl.when(f == last)
    def _():
        for rows in chunks:
            o_ref[rows, :] = _rmsnorm_f32(o_ref[rows, :] + contribution(rows), gf_ref[...])


def _mlp(h, g_mlp, w_up_tiles, w_down, g_final, block_m, row_chunk):
    n_tok, d_model = h.shape
    f_tiles, _, block_f = w_up_tiles.shape
    kernel = functools.partial(_mlp_kernel, row_chunk=row_chunk)
    return pl.pallas_call(
        kernel,
        grid=(n_tok // block_m, f_tiles),
        in_specs=[
            pl.BlockSpec((block_m, d_model), lambda i, f: (i, 0)),
            _resident((1, d_model), lambda i, f: (0, 0)),
            pl.BlockSpec((None, d_model, block_f), lambda i, f: (f, 0, 0)),
            pl.BlockSpec((block_f, d_model), lambda i, f: (f, 0)),
            _resident((1, d_model), lambda i, f: (0, 0)),
        ],
        out_specs=pl.BlockSpec((block_m, d_model), lambda i, f: (i, 0)),
        out_shape=jax.ShapeDtypeStruct((n_tok, d_model), F32),
        scratch_shapes=[pltpu.VMEM((block_m, d_model), BF16)],
        compiler_params=_params(("arbitrary", "arbitrary")),
        name="mlp",
    )(h, g_mlp, w_up_tiles, w_down, g_final)


def kernel(x, norm_mix_g, w_in, w_fourier, w_out, norm_mlp_g, w_up, w_down, norm_final_g):
    batch, seq, d_model = x.shape
    depth = w_in.shape[0]
    fourier_width = w_fourier.shape[1] * w_fourier.shape[2]
    n_heads = (w_in.shape[2] - fourier_width) // (3 * HEAD_DIM)
    d_ff = w_up.shape[2]
    assert w_in.shape[2] == 4 * fourier_width and n_heads * HEAD_DIM == fourier_width
    assert seq % (CLASSES * K_WINDOW) == 0
    assert depth == 1, "the MLP kernel fuses the final RMSNorm, so it handles a single layer"

    h = x.reshape(batch * seq, d_model)
    heads_nat, heads_cls = _in_proj(h, norm_mix_g[0][None, :], w_in[0].astype(BF16),
                                    batch, seq, block_m=1024)
    yf = _fourier_mix(heads_cls, w_fourier[0].astype(BF16))
    ya, w_up_tiles, w_down_b, w_out_b = _attention(
        heads_nat, heads_cls, n_heads,
        weights=((w_up[0], MLP_BLOCK_F), (w_down[0], d_model), (w_out[0], d_model)))
    h = _out_proj(yf.reshape(batch * seq, fourier_width),
                  ya.reshape(batch * seq, n_heads * HEAD_DIM),
                  h, w_out_b.reshape(d_model, d_model), block_m=512)
    h = _mlp(h, norm_mlp_g[0][None, :], w_up_tiles, w_down_b.reshape(d_ff, d_model),
             norm_final_g[None, :], block_m=1024, row_chunk=512)
    return h.reshape(batch, seq, d_model)
```

```python
import functools

import numpy as np
import jax
import jax.numpy as jnp
from jax import lax
from jax.experimental import pallas as pl
from jax.experimental.pallas import tpu as pltpu

F32 = jnp.float32
BF16 = jnp.bfloat16

HEAD_DIM = 128
WINDOW_DILATIONS = ((128, 1), (512, 4), (2048, 16))
DILATIONS = tuple(d for _, d in WINDOW_DILATIONS)
RADIUS = 64
assert all((w // 2) // d == RADIUS for w, d in WINDOW_DILATIONS)
RMS_EPS = 1e-6
MASK_VALUE = -1e30
LOG2_E = 1.4426950408889634

CLASSES = max(DILATIONS)
Q_BLOCK = 128
K_WINDOW = Q_BLOCK + 2 * RADIUS
DFT_RADIX = 8
MLP_BLOCK_F = 512
RESULT_PITCH = 24

VMEM_LIMIT_BYTES = 60 * 1024 * 1024


def _params(semantics):
    return pltpu.CompilerParams(dimension_semantics=semantics, vmem_limit_bytes=VMEM_LIMIT_BYTES)


def _resident(block_shape, index_map):
    return pl.BlockSpec(block_shape, index_map, pipeline_mode=pl.Buffered(1))


def _rmsnorm_f32(x, g):
    inv = lax.rsqrt(jnp.mean(x * x, axis=-1, keepdims=True) + RMS_EPS)
    return x * inv * g


def _in_proj_kernel(x_ref, g_ref, w_ref, nat_ref, cls_ref, u_scr, nat_scr, c4_scr, *,
                    heads_per_block, q_scale):
    j = pl.program_id(1)
    block_m = u_scr.shape[0]
    chunk_heads = 2
    chunk_n = chunk_heads * HEAD_DIM

    @pl.when(j == 0)
    def _():
        u_scr[...] = _rmsnorm_f32(x_ref[...], g_ref[...]).astype(BF16)

    scale = jnp.where(j == 1, q_scale, 1.0)
    for c in range(heads_per_block // chunk_heads):
        acc = jnp.dot(u_scr[...], w_ref[:, c * chunk_n:(c + 1) * chunk_n],
                      preferred_element_type=F32) * scale
        for hh in range(c * chunk_heads, (c + 1) * chunk_heads):
            head = acc[:, (hh % chunk_heads) * HEAD_DIM:(hh % chunk_heads + 1) * HEAD_DIM]
            nat_ref[0, hh, 0] = head.astype(BF16)
            nat_scr[hh] = head
            for r in range(4):
                c4_scr[hh, r] = nat_scr[hh, pl.ds(r, block_m // 4, stride=4), :]
            for r in range(4):
                for a in range(4):
                    piece = c4_scr[hh, r, pl.ds(a, block_m // 16, stride=4), :]
                    cls_ref[0, hh, 4 * a + r] = piece.astype(BF16)


def _in_proj(x2, g, w_in, batch, seq, block_m):
    n_tok, d_model = x2.shape
    in_width = w_in.shape[1]
    block_n = in_width // 4
    heads_per_block = block_n // HEAD_DIM
    tiles_per_batch = seq // block_m
    kernel = functools.partial(_in_proj_kernel, heads_per_block=heads_per_block,
                               q_scale=HEAD_DIM ** -0.5 * LOG2_E)

    def head_block(classes, first_block):
        return pl.BlockSpec((1, heads_per_block, classes, block_m // classes, HEAD_DIM),
                            lambda i, j: (i // tiles_per_batch, jnp.maximum(j - first_block, 0), 0,
                                          i % tiles_per_batch, 0))

    def head_shape(classes, first_block):
        heads = in_width // HEAD_DIM - first_block * heads_per_block
        return jax.ShapeDtypeStruct((batch, heads, classes, seq // classes, HEAD_DIM), BF16)

    return pl.pallas_call(
        kernel,
        grid=(n_tok // block_m, in_width // block_n),
        in_specs=[
            pl.BlockSpec((block_m, d_model), lambda i, j: (i, 0)),
            _resident((1, d_model), lambda i, j: (0, 0)),
            pl.BlockSpec((d_model, block_n), lambda i, j: (0, j)),
        ],
        out_specs=[head_block(1, 1), head_block(CLASSES, 0)],
        out_shape=[head_shape(1, 1), head_shape(CLASSES, 0)],
        scratch_shapes=[pltpu.VMEM((block_m, d_model), BF16),
                        pltpu.VMEM((heads_per_block, block_m, HEAD_DIM), F32),
                        pltpu.VMEM((heads_per_block, 4, block_m // 4, HEAD_DIM), F32)],
        compiler_params=_params(("arbitrary", "arbitrary")),
        name="in_proj",
    )(x2, g, w_in)


def _dft_tables(seq, group_dim):
    part = seq // DFT_RADIX
    sub = CLASSES // DFT_RADIX
    s = np.arange(part, dtype=np.int64)[:, None]
    t_prime = np.array([sub * m + a for a in range(sub) for m in range(seq // CLASSES)],
                       dtype=np.int64)[None, :]
    tabs = []
    for r in range(DFT_RADIX):
        ang = 2.0 * np.pi * ((s * (DFT_RADIX * t_prime + r)) % seq).astype(np.float64) / seq
        tabs.append(np.concatenate([np.cos(ang), -np.sin(ang)], axis=0))
    seq_tab = np.stack(tabs).astype(np.float32)
    c = np.arange(group_dim, dtype=np.int64)
    ang = 2.0 * np.pi * ((c[:, None] * c[None, :]) % group_dim).astype(np.float64) / group_dim
    ortho = 1.0 / np.sqrt(float(seq) * float(group_dim))
    chan_tab = (np.concatenate([np.cos(ang), np.sin(ang)], axis=0) * ortho).astype(np.float32)
    return seq_tab, chan_tab


def _dft4(x0, x1, x2, x3):
    er, ei, fr, fi = x0[0] + x2[0], x0[1] + x2[1], x0[0] - x2[0], x0[1] - x2[1]
    gr, gi, hr, hi = x1[0] + x3[0], x1[1] + x3[1], x1[0] - x3[0], x1[1] - x3[1]
    return (er + gr, ei + gi), (fr + hi, fi - hr), (er - gr, ei - gi), (fr - hi, fi + hr)


def _dft8(x):
    half = 0.5 ** 0.5
    even = _dft4(*[(x[r][0] + x[r + 4][0], x[r][1] + x[r + 4][1]) for r in range(4)])
    d = [(x[r][0] - x[r + 4][0], x[r][1] - x[r + 4][1]) for r in range(4)]
    odd = _dft4(d[0],
                ((d[1][0] + d[1][1]) * half, (d[1][1] - d[1][0]) * half),
                (d[2][1], -d[2][0]),
                ((d[3][1] - d[3][0]) * half, (-d[3][0] - d[3][1]) * half))
    return [pair for pairs in zip(even, odd) for pair in pairs]


def _fourier_kernel(tab_ref, u_ref, chan_ref, wf_ref, y_ref):
    slabs, classes, members, _ = u_ref.shape
    part = classes * members // DFT_RADIX
    g = []
    for r in range(DFT_RADIX):
        u = jnp.concatenate(
            [jnp.concatenate([u_ref[i, DFT_RADIX * a + r] for i in range(slabs)], axis=1)
             for a in range(classes // DFT_RADIX)], axis=0)
        gr = jnp.dot(tab_ref[r], u, preferred_element_type=F32)
        g.append((gr[:part], gr[part:]))
    for q, (zr, zi) in enumerate(_dft8(g)):
        zc = jnp.concatenate([zr, zi], axis=1).astype(BF16)
        real = jnp.dot(zc, chan_ref[...], preferred_element_type=F32).astype(BF16)
        y = jnp.dot(real, wf_ref[0], preferred_element_type=F32)
        y_ref[0, q * part:(q + 1) * part, :] = y.astype(y_ref.dtype)


def _fourier_mix(heads_cls, w_fourier):
    batch, _, classes, members, _ = heads_cls.shape
    seq = classes * members
    part = seq // DFT_RADIX
    n_groups, group_dim, _ = w_fourier.shape
    fw = n_groups * group_dim
    slabs = group_dim // HEAD_DIM
    seq_tab, chan_tab = _dft_tables(seq, group_dim)
    seq_tab = jnp.asarray(seq_tab).astype(BF16)
    chan_tab = jnp.asarray(chan_tab).astype(BF16)

    return pl.pallas_call(
        _fourier_kernel,
        grid=(batch, n_groups),
        in_specs=[
            _resident((DFT_RADIX, 2 * part, part), lambda b, g: (0, 0, 0)),
            pl.BlockSpec((None, slabs, classes, members, HEAD_DIM), lambda b, g: (b, g, 0, 0, 0)),
            _resident((2 * group_dim, group_dim), lambda b, g: (0, 0)),
            pl.BlockSpec((1, group_dim, group_dim), lambda b, g: (g, 0, 0)),
        ],
        out_specs=pl.BlockSpec((1, seq, group_dim), lambda b, g: (b, 0, g)),
        out_shape=jax.ShapeDtypeStruct((batch, seq, fw), BF16),
        compiler_params=_params(("arbitrary", "arbitrary")),
        name="fourier_mix",
    )(seq_tab, heads_cls, chan_tab, w_fourier)


def _attention_blocks(dil, seq):
    sub = CLASSES // dil
    q_rows, k_rows, radius = Q_BLOCK // sub, K_WINDOW // sub, RADIUS // sub
    members = seq // CLASSES
    for r in range(dil):
        for m0 in range(0, members, q_rows):
            w0 = min(max(m0 - radius, 0), members - k_rows)
            yield ((m0 - w0) // radius,
                   [(dil * a + r, m0, q_rows) for a in range(sub)],
                   [(dil * a + r, w0, k_rows) for a in range(sub)],
                   [(a * q_rows, q_rows, RESULT_PITCH * m0 + dil * a + r) for a in range(sub)])


def _attention_kernel(slopes_ref, q1_ref, k1_ref, v1_ref, qc_ref, kc_ref, vc_ref,
                      w0_ref, w1_ref, w2_ref,
                      o_ref, w0_out, w1_out, w2_out,
                      bias_scr, acc4_scr, m4_scr, den4_scr, acc16_scr, m16_scr, den16_scr):
    seq = o_ref.shape[0]
    slope = slopes_ref[pl.program_id(1)]

    for src, dst in ((w0_ref, w0_out), (w1_ref, w1_out), (w2_ref, w2_out)):
        tiles, _, width = dst.shape
        for t in range(tiles):
            dst[t] = src[:, t * width:(t + 1) * width].astype(dst.dtype)

    row = lax.broadcasted_iota(jnp.int32, (Q_BLOCK, K_WINDOW), 0)
    col = lax.broadcasted_iota(jnp.int32, (Q_BLOCK, K_WINDOW), 1)
    for gi, dil in enumerate(DILATIONS):
        sub = CLASSES // dil if dil > 1 else 1
        q_rows, k_rows, radius = Q_BLOCK // sub, K_WINDOW // sub, RADIUS // sub
        for variant in range(3):
            rel = sub * (col % k_rows - row % q_rows - variant * radius) + (col // k_rows - row // q_rows)
            rel = jnp.abs(rel)
            penalty = (slope * LOG2_E) * (dil * rel).astype(F32)
            bias_scr[gi * 3 + variant] = jnp.where(rel <= RADIUS, -penalty, MASK_VALUE)

    ones = jnp.ones((K_WINDOW, HEAD_DIM), BF16)

    def gather(ref, pieces):
        parts = [ref[c, r0:r0 + n, :] for c, r0, n in pieces]
        return parts[0] if len(parts) == 1 else jnp.concatenate(parts, axis=0)

    def softmax_block(gi, variant, q, k, v):
        s = lax.dot_general(q, k, (((1,), (1,)), ((), ())), preferred_element_type=F32)
        s = s + bias_scr[gi * 3 + variant]
        m = jnp.max(s, axis=1, keepdims=True)
        p = jnp.exp2(s - m).astype(BF16)
        acc = jnp.dot(p, jnp.concatenate([v, ones], axis=1), preferred_element_type=F32)
        return acc[:, :HEAD_DIM], jnp.broadcast_to(m, (Q_BLOCK, HEAD_DIM)), acc[:, HEAD_DIM:]

    def natural_rows(scr, qs):
        base = qs // CLASSES * RESULT_PITCH
        return jnp.concatenate([scr[base + i * RESULT_PITCH:base + i * RESULT_PITCH + CLASSES, :]
                                for i in range(Q_BLOCK // CLASSES)], axis=0)

    for gi, scratch in ((2, (acc16_scr, m16_scr, den16_scr)), (1, (acc4_scr, m4_scr, den4_scr))):
        for variant, q_pieces, k_pieces, out_pieces in _attention_blocks(DILATIONS[gi], seq):
            results = softmax_block(gi, variant, gather(qc_ref, q_pieces), gather(kc_ref, k_pieces),
                                    gather(vc_ref, k_pieces))
            for first, rows, start in out_pieces:
                for scr, value in zip(scratch, results):
                    scr[pl.ds(start, rows, stride=RESULT_PITCH), :] = value[first:first + rows]

    for qs in range(0, seq, Q_BLOCK):
        ws = min(max(qs - RADIUS, 0), seq - K_WINDOW)
        num, m, den = softmax_block(0, (qs - ws) // RADIUS, q1_ref[0, qs:qs + Q_BLOCK, :],
                                    k1_ref[0, ws:ws + K_WINDOW, :], v1_ref[0, ws:ws + K_WINDOW, :])
        m4, m16 = natural_rows(m4_scr, qs), natural_rows(m16_scr, qs)
        top = jnp.maximum(m, jnp.maximum(m4, m16))
        e1, e4, e16 = jnp.exp2(m - top), jnp.exp2(m4 - top), jnp.exp2(m16 - top)
        num = e1 * num + e4 * natural_rows(acc4_scr, qs) + e16 * natural_rows(acc16_scr, qs)
        den = e1 * den + e4 * natural_rows(den4_scr, qs) + e16 * natural_rows(den16_scr, qs)
        o_ref[qs:qs + Q_BLOCK, :] = (num * pl.reciprocal(den, approx=True)).astype(o_ref.dtype)


def _attention(heads_nat, heads_cls, n_heads, weights):
    batch, _, _, seq, _ = heads_nat.shape
    operands, in_specs = [], []
    for heads in (heads_nat, heads_cls):
        first = heads.shape[1] - 3 * n_heads
        for which in range(3):
            operands.append(heads)
            in_specs.append(pl.BlockSpec((None, None) + heads.shape[2:],
                                         lambda b, h, which=which, first=first:
                                         (b, first + which * n_heads + h, 0, 0, 0)))
    steps = batch * n_heads
    w_operands, w_in_specs, w_out_specs, w_out_shapes = [], [], [], []
    for w, width in weights:
        rows, cols = w.shape
        w_operands.append(w.reshape(steps, rows // steps, cols))
        w_in_specs.append(pl.BlockSpec((None, rows // steps, cols),
                                       lambda b, h: (b * n_heads + h, 0, 0)))
        w_out_specs.append(pl.BlockSpec((cols // width, rows // steps, width),
                                        lambda b, h: (0, b * n_heads + h, 0)))
        w_out_shapes.append(jax.ShapeDtypeStruct((cols // width, rows, width), BF16))
    slopes = jnp.asarray(2.0 ** (-8.0 * (np.arange(n_heads) + 1) / n_heads), dtype=F32)
    result = lambda: pltpu.VMEM((seq // CLASSES * RESULT_PITCH, HEAD_DIM), F32)
    return pl.pallas_call(
        _attention_kernel,
        grid=(batch, n_heads),
        in_specs=[pl.BlockSpec(memory_space=pltpu.SMEM)] + in_specs + w_in_specs,
        out_specs=[pl.BlockSpec((None, seq, HEAD_DIM), lambda b, h: (b, 0, h))] + w_out_specs,
        out_shape=[jax.ShapeDtypeStruct((batch, seq, n_heads * HEAD_DIM), BF16)] + w_out_shapes,
        scratch_shapes=[pltpu.VMEM((3 * len(DILATIONS), Q_BLOCK, K_WINDOW), F32)]
        + [result() for _ in range(6)],
        compiler_params=_params(("arbitrary", "arbitrary")),
        name="dilated_attention",
    )(slopes, *operands, *w_operands)


def _out_proj_kernel(yf_ref, ya_ref, x_ref, w_ref, h_ref):
    fw = yf_ref.shape[1]
    acc = jnp.dot(yf_ref[...], w_ref[:fw, :], preferred_element_type=F32)
    acc = acc + jnp.dot(ya_ref[...], w_ref[fw:, :], preferred_element_type=F32)
    h_ref[...] = x_ref[...] + acc


def _out_proj(yf, ya, x2, w_out, block_m):
    n_tok, d_model = x2.shape
    fw, aw = yf.shape[1], ya.shape[1]
    return pl.pallas_call(
        _out_proj_kernel,
        grid=(n_tok // block_m,),
        in_specs=[
            pl.BlockSpec((block_m, fw), lambda i: (i, 0)),
            pl.BlockSpec((block_m, aw), lambda i: (i, 0)),
            pl.BlockSpec((block_m, d_model), lambda i: (i, 0)),
            _resident((fw + aw, d_model), lambda i: (0, 0)),
        ],
        out_specs=pl.BlockSpec((block_m, d_model), lambda i: (i, 0)),
        out_shape=jax.ShapeDtypeStruct((n_tok, d_model), F32),
        compiler_params=_params(("arbitrary",)),
        name="out_proj",
    )(yf, ya, x2, w_out)


def _mlp_kernel(h_ref, g_ref, wu_ref, wd_ref, gf_ref, o_ref, u_scr, *, row_chunk):
    f = pl.program_id(1)
    last = pl.num_programs(1) - 1

    @pl.when(f == 0)
    def _():
        u_scr[...] = _rmsnorm_f32(h_ref[...], g_ref[...]).astype(BF16)

    def contribution(rows):
        a = jnp.dot(u_scr[rows, :], wu_ref[...], preferred_element_type=F32)
        a = jnp.square(jnp.maximum(a, 0.0)).astype(BF16)
        return jnp.dot(a, wd_ref[...], preferred_element_type=F32)

    chunks = [slice(c, c + row_chunk) for c in range(0, o_ref.shape[0], row_chunk)]

    @pl.when(f == 0)
    def _():
        for rows in chunks:
            o_ref[rows, :] = h_ref[rows, :] + contribution(rows)

    @pl.when(jnp.logical_and(f > 0, f < last))
    def _():
        for rows in chunks:
            o_ref[rows, :] += contribution(rows)

    @pl.when(f == last)
    def _():
        for rows in chunks:
            o_ref[rows, :] = _rmsnorm_f32(o_ref[rows, :] + contribution(rows), gf_ref[...])


def _mlp(h, g_mlp, w_up_tiles, w_down, g_final, block_m, row_chunk):
    n_tok, d_model = h.shape
    f_tiles, _, block_f = w_up_tiles.shape
    kernel = functools.partial(_mlp_kernel, row_chunk=row_chunk)
    return pl.pallas_call(
        kernel,
        grid=(n_tok // block_m, f_tiles),
        in_specs=[
            pl.BlockSpec((block_m, d_model), lambda i, f: (i, 0)),
            _resident((1, d_model), lambda i, f: (0, 0)),
            pl.BlockSpec((None, d_model, block_f), lambda i, f: (f, 0, 0)),
            pl.BlockSpec((block_f, d_model), lambda i, f: (f, 0)),
            _resident((1, d_model), lambda i, f: (0, 0)),
        ],
        out_specs=pl.BlockSpec((block_m, d_model), lambda i, f: (i, 0)),
        out_shape=jax.ShapeDtypeStruct((n_tok, d_model), F32),
        scratch_shapes=[pltpu.VMEM((block_m, d_model), BF16)],
        compiler_params=_params(("arbitrary", "arbitrary")),
        name="mlp",
    )(h, g_mlp, w_up_tiles, w_down, g_final)


def kernel(x, norm_mix_g, w_in, w_fourier, w_out, norm_mlp_g, w_up, w_down, norm_final_g):
    batch, seq, d_model = x.shape
    depth = w_in.shape[0]
    fourier_width = w_fourier.shape[1] * w_fourier.shape[2]
    n_heads = (w_in.shape[2] - fourier_width) // (3 * HEAD_DIM)
    d_ff = w_up.shape[2]
    assert w_in.shape[2] == 4 * fourier_width and n_heads * HEAD_DIM == fourier_width
    assert seq % (CLASSES * K_WINDOW) == 0
    assert depth == 1, "the MLP kernel fuses the final RMSNorm, so it handles a single layer"

    h = x.reshape(batch * seq, d_model)
    heads_nat, heads_cls = _in_proj(h, norm_mix_g[0][None, :], w_in[0].astype(BF16),
                                    batch, seq, block_m=1024)
    yf = _fourier_mix(heads_cls, w_fourier[0].astype(BF16))
    ya, w_up_tiles, w_down_b, w_out_b = _attention(
        heads_nat, heads_cls, n_heads,
        weights=((w_up[0], MLP_BLOCK_F), (w_down[0], d_model), (w_out[0], d_model)))
    h = _out_proj(yf.reshape(batch * seq, fourier_width),
                  ya.reshape(batch * seq, n_heads * HEAD_DIM),
                  h, w_out_b.reshape(d_model, d_model), block_m=512)
    h = _mlp(h, norm_mlp_g[0][None, :], w_up_tiles, w_down_b.reshape(d_ff, d_model),
             norm_final_g[None, :], block_m=1024, row_chunk=512)
    return h.reshape(batch, seq, d_model)
```

```python
import functools

import numpy as np
import jax
import jax.numpy as jnp
from jax import lax
from jax.experimental import pallas as pl
from jax.experimental.pallas import tpu as pltpu

F32 = jnp.float32
BF16 = jnp.bfloat16

HEAD_DIM = 128
WINDOW_DILATIONS = ((128, 1), (512, 4), (2048, 16))
DILATIONS = tuple(d for _, d in WINDOW_DILATIONS)
RADIUS = 64
assert all((w // 2) // d == RADIUS for w, d in WINDOW_DILATIONS)
RMS_EPS = 1e-6
MASK_VALUE = -1e30
LOG2_E = 1.4426950408889634

CLASSES = max(DILATIONS)
Q_BLOCK = 128
K_WINDOW = Q_BLOCK + 2 * RADIUS
DFT_RADIX = 8
MLP_BLOCK_F = 512
RESULT_PITCH = 24

VMEM_LIMIT_BYTES = 60 * 1024 * 1024


def _params(semantics):
    return pltpu.CompilerParams(dimension_semantics=semantics, vmem_limit_bytes=VMEM_LIMIT_BYTES)


def _resident(block_shape, index_map):
    return pl.BlockSpec(block_shape, index_map, pipeline_mode=pl.Buffered(1))


def _rmsnorm_f32(x, g):
    inv = lax.rsqrt(jnp.mean(x * x, axis=-1, keepdims=True) + RMS_EPS)
    return x * inv * g


def _in_proj_kernel(x_ref, g_ref, w_ref, nat_ref, cls_ref, u_scr, nat_scr, c4_scr, *,
                    heads_per_block, q_scale):
    j = pl.program_id(1)
    block_m = u_scr.shape[0]
    chunk_heads = 2
    chunk_n = chunk_heads * HEAD_DIM

    @pl.when(j == 0)
    def _():
        u_scr[...] = _rmsnorm_f32(x_ref[...], g_ref[...]).astype(BF16)

    scale = jnp.where(j == 1, q_scale, 1.0)
    for c in range(heads_per_block // chunk_heads):
        acc = jnp.dot(u_scr[...], w_ref[:, c * chunk_n:(c + 1) * chunk_n],
                      preferred_element_type=F32) * scale
        for hh in range(c * chunk_heads, (c + 1) * chunk_heads):
            head = acc[:, (hh % chunk_heads) * HEAD_DIM:(hh % chunk_heads + 1) * HEAD_DIM]
            nat_ref[0, hh, 0] = head.astype(BF16)
            nat_scr[hh] = head
            for r in range(4):
                c4_scr[hh, r] = nat_scr[hh, pl.ds(r, block_m // 4, stride=4), :]
            for r in range(4):
                for a in range(4):
                    piece = c4_scr[hh, r, pl.ds(a, block_m // 16, stride=4), :]
                    cls_ref[0, hh, 4 * a + r] = piece.astype(BF16)


def _in_proj(x2, g, w_in, batch, seq, block_m):
    n_tok, d_model = x2.shape
    in_width = w_in.shape[1]
    block_n = in_width // 4
    heads_per_block = block_n // HEAD_DIM
    tiles_per_batch = seq // block_m
    kernel = functools.partial(_in_proj_kernel, heads_per_block=heads_per_block,
                               q_scale=HEAD_DIM ** -0.5 * LOG2_E)

    def head_block(classes, first_block):
        return pl.BlockSpec((1, heads_per_block, classes, block_m // classes, HEAD_DIM),
                            lambda i, j: (i // tiles_per_batch, jnp.maximum(j - first_block, 0), 0,
                                          i % tiles_per_batch, 0))

    def head_shape(classes, first_block):
        heads = in_width // HEAD_DIM - first_block * heads_per_block
        return jax.ShapeDtypeStruct((batch, heads, classes, seq // classes, HEAD_DIM), BF16)

    return pl.pallas_call(
        kernel,
        grid=(n_tok // block_m, in_width // block_n),
        in_specs=[
            pl.BlockSpec((block_m, d_model), lambda i, j: (i, 0)),
            _resident((1, d_model), lambda i, j: (0, 0)),
            pl.BlockSpec((d_model, block_n), lambda i, j: (0, j)),
        ],
        out_specs=[head_block(1, 1), head_block(CLASSES, 0)],
        out_shape=[head_shape(1, 1), head_shape(CLASSES, 0)],
        scratch_shapes=[pltpu.VMEM((block_m, d_model), BF16),
                        pltpu.VMEM((heads_per_block, block_m, HEAD_DIM), F32),
                        pltpu.VMEM((heads_per_block, 4, block_m // 4, HEAD_DIM), F32)],
        compiler_params=_params(("arbitrary", "arbitrary")),
        name="in_proj",
    )(x2, g, w_in)


def _dft_tables(seq, group_dim):
    part = seq // DFT_RADIX
    sub = CLASSES // DFT_RADIX
    s = np.arange(part, dtype=np.int64)[:, None]
    t_prime = np.array([sub * m + a for a in range(sub) for m in range(seq // CLASSES)],
                       dtype=np.int64)[None, :]
    tabs = []
    for r in range(DFT_RADIX):
        ang = 2.0 * np.pi * ((s * (DFT_RADIX * t_prime + r)) % seq).astype(np.float64) / seq
        tabs.append(np.concatenate([np.cos(ang), -np.sin(ang)], axis=0))
    seq_tab = np.stack(tabs).astype(np.float32)
    c = np.arange(group_dim, dtype=np.int64)
    ang = 2.0 * np.pi * ((c[:, None] * c[None, :]) % group_dim).astype(np.float64) / group_dim
    ortho = 1.0 / np.sqrt(float(seq) * float(group_dim))
    chan_tab = (np.concatenate([np.cos(ang), np.sin(ang)], axis=0) * ortho).astype(np.float32)
    return seq_tab, chan_tab


def _dft4(x0, x1, x2, x3):
    er, ei, fr, fi = x0[0] + x2[0], x0[1] + x2[1], x0[0] - x2[0], x0[1] - x2[1]
    gr, gi, hr, hi = x1[0] + x3[0], x1[1] + x3[1], x1[0] - x3[0], x1[1] - x3[1]
    return (er + gr, ei + gi), (fr + hi, fi - hr), (er - gr, ei - gi), (fr - hi, fi + hr)


def _dft8(x):
    half = 0.5 ** 0.5
    even = _dft4(*[(x[r][0] + x[r + 4][0], x[r][1] + x[r + 4][1]) for r in range(4)])
    d = [(x[r][0] - x[r + 4][0], x[r][1] - x[r + 4][1]) for r in range(4)]
    odd = _dft4(d[0],
                ((d[1][0] + d[1][1]) * half, (d[1][1] - d[1][0]) * half),
                (d[2][1], -d[2][0]),
                ((d[3][1] - d[3][0]) * half, (-d[3][0] - d[3][1]) * half))
    return [pair for pairs in zip(even, odd) for pair in pairs]


def _fourier_kernel(tab_ref, u_ref, chan_ref, wf_ref, y_ref):
    n_slabs, classes, members, _ = u_ref.shape
    groups, group_dim, _ = wf_ref.shape
    slabs = n_slabs // groups
    part = classes * members // DFT_RADIX
    def partial_dfts(grp):
        g = []
        for r in range(DFT_RADIX):
            u = jnp.concatenate(
                [jnp.concatenate([u_ref[grp * slabs + i, DFT_RADIX * a + r] for i in range(slabs)],
                                 axis=1)
                 for a in range(classes // DFT_RADIX)], axis=0)
            gr = jnp.dot(tab_ref[r], u, preferred_element_type=F32)
            g.append((gr[:part], gr[part:]))
        return g

    all_g = [partial_dfts(grp) for grp in range(groups)]
    for grp in range(groups):
        g = all_g[grp]
        chan_w = jnp.dot(chan_ref[...], wf_ref[grp], preferred_element_type=F32).astype(BF16)
        for q, (zr, zi) in enumerate(_dft8(g)):
            zc = jnp.concatenate([zr, zi], axis=1).astype(BF16)
            y = jnp.dot(zc, chan_w, preferred_element_type=F32)
            y_ref[0, q * part:(q + 1) * part, grp * group_dim:(grp + 1) * group_dim] = (
                y.astype(y_ref.dtype))


def _fourier_mix(heads_cls, w_fourier):
    batch, _, classes, members, _ = heads_cls.shape
    seq = classes * members
    part = seq // DFT_RADIX
    n_groups, group_dim, _ = w_fourier.shape
    fw = n_groups * group_dim
    slabs = group_dim // HEAD_DIM
    per_step = 2
    seq_tab, chan_tab = _dft_tables(seq, group_dim)
    seq_tab = jnp.asarray(seq_tab).astype(BF16)
    chan_tab = jnp.asarray(chan_tab).astype(BF16)

    return pl.pallas_call(
        _fourier_kernel,
        grid=(batch, n_groups // per_step),
        in_specs=[
            _resident((DFT_RADIX, 2 * part, part), lambda b, g: (0, 0, 0)),
            pl.BlockSpec((None, per_step * slabs, classes, members, HEAD_DIM),
                         lambda b, g: (b, g, 0, 0, 0)),
            _resident((2 * group_dim, group_dim), lambda b, g: (0, 0)),
            pl.BlockSpec((per_step, group_dim, group_dim), lambda b, g: (g, 0, 0)),
        ],
        out_specs=pl.BlockSpec((1, seq, per_step * group_dim), lambda b, g: (b, 0, g)),
        out_shape=jax.ShapeDtypeStruct((batch, seq, fw), BF16),
        compiler_params=_params(("arbitrary", "arbitrary")),
        name="fourier_mix",
    )(seq_tab, heads_cls, chan_tab, w_fourier)


def _attention_blocks(dil, seq):
    sub = CLASSES // dil
    q_rows, k_rows, radius = Q_BLOCK // sub, K_WINDOW // sub, RADIUS // sub
    members = seq // CLASSES
    for r in range(dil):
        for m0 in range(0, members, q_rows):
            w0 = min(max(m0 - radius, 0), members - k_rows)
            yield ((m0 - w0) // radius,
                   [(dil * a + r, m0, q_rows) for a in range(sub)],
                   [(dil * a + r, w0, k_rows) for a in range(sub)],
                   [(a * q_rows, q_rows, RESULT_PITCH * m0 + dil * a + r) for a in range(sub)])


def _attention_kernel(slopes_ref, q1_ref, k1_ref, v1_ref, qc_ref, kc_ref, vc_ref,
                      w0_ref, w1_ref, w2_ref,
                      o_ref, w0_out, w1_out, w2_out,
                      bias_scr, acc4_scr, m4_scr, den4_scr, acc16_scr, m16_scr, den16_scr):
    seq = o_ref.shape[0]
    slope = slopes_ref[pl.program_id(1)]

    for src, dst in ((w0_ref, w0_out), (w1_ref, w1_out), (w2_ref, w2_out)):
        tiles, _, width = dst.shape
        for t in range(tiles):
            dst[t] = src[:, t * width:(t + 1) * width].astype(dst.dtype)

    row = lax.broadcasted_iota(jnp.int32, (Q_BLOCK, K_WINDOW), 0)
    col = lax.broadcasted_iota(jnp.int32, (Q_BLOCK, K_WINDOW), 1)
    for gi, dil in enumerate(DILATIONS):
        sub = CLASSES // dil if dil > 1 else 1
        q_rows, k_rows, radius = Q_BLOCK // sub, K_WINDOW // sub, RADIUS // sub
        for variant in range(3):
            rel = sub * (col % k_rows - row % q_rows - variant * radius) + (col // k_rows - row // q_rows)
            rel = jnp.abs(rel)
            penalty = (slope * LOG2_E) * (dil * rel).astype(F32)
            bias_scr[gi * 3 + variant] = jnp.where(rel <= RADIUS, -penalty, MASK_VALUE)

    ones = jnp.ones((K_WINDOW, HEAD_DIM), BF16)

    def gather(ref, pieces):
        parts = [ref[c, r0:r0 + n, :] for c, r0, n in pieces]
        return parts[0] if len(parts) == 1 else jnp.concatenate(parts, axis=0)

    def softmax_block(gi, variant, q, k, v):
        s = lax.dot_general(q, k, (((1,), (1,)), ((), ())), preferred_element_type=F32)
        s = s + bias_scr[gi * 3 + variant]
        m = jnp.max(s, axis=1, keepdims=True)
        p = jnp.exp2(s - m).astype(BF16)
        acc = jnp.dot(p, jnp.concatenate([v, ones], axis=1), preferred_element_type=F32)
        return acc[:, :HEAD_DIM], jnp.broadcast_to(m, (Q_BLOCK, HEAD_DIM)), acc[:, HEAD_DIM:]

    def natural_rows(scr, qs):
        base = qs // CLASSES * RESULT_PITCH
        return jnp.concatenate([scr[base + i * RESULT_PITCH:base + i * RESULT_PITCH + CLASSES, :]
                                for i in range(Q_BLOCK // CLASSES)], axis=0)

    for gi, scratch in ((2, (acc16_scr, m16_scr, den16_scr)), (1, (acc4_scr, m4_scr, den4_scr))):
        for variant, q_pieces, k_pieces, out_pieces in _attention_blocks(DILATIONS[gi], seq):
            results = softmax_block(gi, variant, gather(qc_ref, q_pieces), gather(kc_ref, k_pieces),
                                    gather(vc_ref, k_pieces))
            for first, rows, start in out_pieces:
                for scr, value in zip(scratch, results):
                    scr[pl.ds(start, rows, stride=RESULT_PITCH), :] = value[first:first + rows]

    for qs in range(0, seq, Q_BLOCK):
        ws = min(max(qs - RADIUS, 0), seq - K_WINDOW)
        num, m, den = softmax_block(0, (qs - ws) // RADIUS, q1_ref[0, qs:qs + Q_BLOCK, :],
                                    k1_ref[0, ws:ws + K_WINDOW, :], v1_ref[0, ws:ws + K_WINDOW, :])
        m4, m16 = natural_rows(m4_scr, qs), natural_rows(m16_scr, qs)
        top = jnp.maximum(m, jnp.maximum(m4, m16))
        e1, e4, e16 = jnp.exp2(m - top), jnp.exp2(m4 - top), jnp.exp2(m16 - top)
        num = e1 * num + e4 * natural_rows(acc4_scr, qs) + e16 * natural_rows(acc16_scr, qs)
        den = e1 * den + e4 * natural_rows(den4_scr, qs) + e16 * natural_rows(den16_scr, qs)
        o_ref[qs:qs + Q_BLOCK, :] = (num * pl.reciprocal(den, approx=True)).astype(o_ref.dtype)


def _attention(heads_nat, heads_cls, n_heads, weights):
    batch, _, _, seq, _ = heads_nat.shape
    operands, in_specs = [], []
    for heads in (heads_nat, heads_cls):
        first = heads.shape[1] - 3 * n_heads
        for which in range(3):
            operands.append(heads)
            in_specs.append(pl.BlockSpec((None, None) + heads.shape[2:],
                                         lambda b, h, which=which, first=first:
                                         (b, first + which * n_heads + h, 0, 0, 0)))
    steps = batch * n_heads
    w_operands, w_in_specs, w_out_specs, w_out_shapes = [], [], [], []
    for w, width in weights:
        rows, cols = w.shape
        w_operands.append(w.reshape(steps, rows // steps, cols))
        w_in_specs.append(pl.BlockSpec((None, rows // steps, cols),
                                       lambda b, h: (b * n_heads + h, 0, 0)))
        w_out_specs.append(pl.BlockSpec((cols // width, rows // steps, width),
                                        lambda b, h: (0, b * n_heads + h, 0)))
        w_out_shapes.append(jax.ShapeDtypeStruct((cols // width, rows, width), BF16))
    slopes = jnp.asarray(2.0 ** (-8.0 * (np.arange(n_heads) + 1) / n_heads), dtype=F32)
    result = lambda: pltpu.VMEM((seq // CLASSES * RESULT_PITCH, HEAD_DIM), F32)
    return pl.pallas_call(
        _attention_kernel,
        grid=(batch, n_heads),
        in_specs=[pl.BlockSpec(memory_space=pltpu.SMEM)] + in_specs + w_in_specs,
        out_specs=[pl.BlockSpec((None, seq, HEAD_DIM), lambda b, h: (b, 0, h))] + w_out_specs,
        out_shape=[jax.ShapeDtypeStruct((batch, seq, n_heads * HEAD_DIM), BF16)] + w_out_shapes,
        scratch_shapes=[pltpu.VMEM((3 * len(DILATIONS), Q_BLOCK, K_WINDOW), F32)]
        + [result() for _ in range(6)],
        compiler_params=_params(("arbitrary", "arbitrary")),
        name="dilated_attention",
    )(slopes, *operands, *w_operands)


def _out_proj_kernel(yf_ref, ya_ref, x_ref, w_ref, h_ref):
    fw = yf_ref.shape[1]
    acc = jnp.dot(yf_ref[...], w_ref[:fw, :], preferred_element_type=F32)
    acc = acc + jnp.dot(ya_ref[...], w_ref[fw:, :], preferred_element_type=F32)
    h_ref[...] = x_ref[...] + acc


def _out_proj(yf, ya, x2, w_out, block_m):
    n_tok, d_model = x2.shape
    fw, aw = yf.shape[1], ya.shape[1]
    return pl.pallas_call(
        _out_proj_kernel,
        grid=(n_tok // block_m,),
        in_specs=[
            pl.BlockSpec((block_m, fw), lambda i: (i, 0)),
            pl.BlockSpec((block_m, aw), lambda i: (i, 0)),
            pl.BlockSpec((block_m, d_model), lambda i: (i, 0)),
            _resident((fw + aw, d_model), lambda i: (0, 0)),
        ],
        out_specs=pl.BlockSpec((block_m, d_model), lambda i: (i, 0)),
        out_shape=jax.ShapeDtypeStruct((n_tok, d_model), F32),
        compiler_params=_params(("arbitrary",)),
        name="out_proj",
    )(yf, ya, x2, w_out)


def _mlp_kernel(h_ref, g_ref, wu_ref, wd_ref, gf_ref, o_ref, u_scr, *, row_chunk):
    f = pl.program_id(1)
    last = pl.num_programs(1) - 1

    @pl.when(f == 0)
    def _():
        u_scr[...] = _rmsnorm_f32(h_ref[...], g_ref[...]).astype(BF16)

    def contribution(rows):
        a = jnp.dot(u_scr[rows, :], wu_ref[...], preferred_element_type=F32)
        a = jnp.square(jnp.maximum(a, 0.0)).astype(BF16)
        return jnp.dot(a, wd_ref[...], preferred_element_type=F32)

    chunks = [slice(c, c + row_chunk) for c in range(0, o_ref.shape[0], row_chunk)]

    @pl.when(f == 0)
    def _():
        for rows in chunks:
            o_ref[rows, :] = h_ref[rows, :] + contribution(rows)

    @pl.when(jnp.logical_and(f > 0, f < last))
    def _():
        for rows in chunks:
            o_ref[rows, :] += contribution(rows)

    @pl.when(f == last)
    def _():
        for rows in chunks:
            o_ref[rows, :] = _rmsnorm_f32(o_ref[rows, :] + contribution(rows), gf_ref[...])


def _mlp(h, g_mlp, w_up_tiles, w_down, g_final, block_m, row_chunk):
    n_tok, d_model = h.shape
    f_tiles, _, block_f = w_up_tiles.shape
    kernel = functools.partial(_mlp_kernel, row_chunk=row_chunk)
    return pl.pallas_call(
        kernel,
        grid=(n_tok // block_m, f_tiles),
        in_specs=[
            pl.BlockSpec((block_m, d_model), lambda i, f: (i, 0)),
            _resident((1, d_model), lambda i, f: (0, 0)),
            pl.BlockSpec((None, d_model, block_f), lambda i, f: (f, 0, 0)),
            pl.BlockSpec((block_f, d_model), lambda i, f: (f, 0)),
            _resident((1, d_model), lambda i, f: (0, 0)),
        ],
        out_specs=pl.BlockSpec((block_m, d_model), lambda i, f: (i, 0)),
        out_shape=jax.ShapeDtypeStruct((n_tok, d_model), F32),
        scratch_shapes=[pltpu.VMEM((block_m, d_model), BF16)],
        compiler_params=_params(("arbitrary", "arbitrary")),
        name="mlp",
    )(h, g_mlp, w_up_tiles, w_down, g_final)


def kernel(x, norm_mix_g, w_in, w_fourier, w_out, norm_mlp_g, w_up, w_down, norm_final_g):
    batch, seq, d_model = x.shape
    depth = w_in.shape[0]
    fourier_width = w_fourier.shape[1] * w_fourier.shape[2]
    n_heads = (w_in.shape[2] - fourier_width) // (3 * HEAD_DIM)
    d_ff = w_up.shape[2]
    assert w_in.shape[2] == 4 * fourier_width and n_heads * HEAD_DIM == fourier_width
    assert seq % (CLASSES * K_WINDOW) == 0
    assert depth == 1, "the MLP kernel fuses the final RMSNorm, so it handles a single layer"

    h = x.reshape(batch * seq, d_model)
    heads_nat, heads_cls = _in_proj(h, norm_mix_g[0][None, :], w_in[0].astype(BF16),
                                    batch, seq, block_m=1024)
    yf = _fourier_mix(heads_cls, w_fourier[0].astype(BF16))
    ya, w_up_tiles, w_down_b, w_out_b = _attention(
        heads_nat, heads_cls, n_heads,
        weights=((w_up[0], MLP_BLOCK_F), (w_down[0], d_model), (w_out[0], d_model)))
    h = _out_proj(yf.reshape(batch * seq, fourier_width),
                  ya.reshape(batch * seq, n_heads * HEAD_DIM),
                  h, w_out_b.reshape(d_model, d_model), block_m=512)
    h = _mlp(h, norm_mlp_g[0][None, :], w_up_tiles, w_down_b.reshape(d_ff, d_model),
             norm_final_g[None, :], block_m=1024, row_chunk=512)
    return h.reshape(batch, seq, d_model)
```

```python
import functools

import numpy as np
import jax
import jax.numpy as jnp
from jax import lax
from jax.experimental import pallas as pl
from jax.experimental.pallas import tpu as pltpu

F32 = jnp.float32
BF16 = jnp.bfloat16

HEAD_DIM = 128
WINDOW_DILATIONS = ((128, 1), (512, 4), (2048, 16))
DILATIONS = tuple(d for _, d in WINDOW_DILATIONS)
RADIUS = 64
assert all((w // 2) // d == RADIUS for w, d in WINDOW_DILATIONS)
RMS_EPS = 1e-6
MASK_VALUE = -1e30
LOG2_E = 1.4426950408889634

CLASSES = max(DILATIONS)
Q_BLOCK = 128
K_WINDOW = Q_BLOCK + 2 * RADIUS
DFT_RADIX = 8
MLP_BLOCK_F = 512
RESULT_PITCH = 24

VMEM_LIMIT_BYTES = 60 * 1024 * 1024


def _params(semantics):
    return pltpu.CompilerParams(dimension_semantics=semantics, vmem_limit_bytes=VMEM_LIMIT_BYTES)


def _resident(block_shape, index_map):
    return pl.BlockSpec(block_shape, index_map, pipeline_mode=pl.Buffered(1))


def _rmsnorm_f32(x, g):
    inv = lax.rsqrt(jnp.mean(x * x, axis=-1, keepdims=True) + RMS_EPS)
    return x * inv * g


def _in_proj_kernel(x_ref, g_ref, w_ref, nat_ref, cls_ref, u_scr, nat_scr, c4_scr, *,
                    heads_per_block, q_scale):
    j = pl.program_id(1)
    block_m = u_scr.shape[0]
    chunk_heads = 2
    chunk_n = chunk_heads * HEAD_DIM

    def project(scale):
        for c in range(heads_per_block // chunk_heads):
            acc = jnp.dot(u_scr[...], w_ref[:, c * chunk_n:(c + 1) * chunk_n],
                          preferred_element_type=F32)
            if scale is not None:
                acc = acc * scale
            for hh in range(c * chunk_heads, (c + 1) * chunk_heads):
                head = acc[:, (hh % chunk_heads) * HEAD_DIM:(hh % chunk_heads + 1) * HEAD_DIM]
                nat_ref[0, hh, 0] = head.astype(BF16)
                nat_scr[hh] = head
                for r in range(4):
                    c4_scr[hh, r] = nat_scr[hh, pl.ds(r, block_m // 4, stride=4), :]
                for r in range(4):
                    for a in range(4):
                        piece = c4_scr[hh, r, pl.ds(a, block_m // 16, stride=4), :]
                        cls_ref[0, hh, 4 * a + r] = piece.astype(BF16)

    @pl.when(j == 0)
    def _():
        u_scr[...] = _rmsnorm_f32(x_ref[...], g_ref[...]).astype(BF16)
        project(None)

    @pl.when(j == 1)
    def _():
        project(q_scale)

    @pl.when(j > 1)
    def _():
        project(None)


def _in_proj(x2, g, w_in, batch, seq, block_m):
    n_tok, d_model = x2.shape
    in_width = w_in.shape[1]
    block_n = in_width // 4
    heads_per_block = block_n // HEAD_DIM
    tiles_per_batch = seq // block_m
    kernel = functools.partial(_in_proj_kernel, heads_per_block=heads_per_block,
                               q_scale=HEAD_DIM ** -0.5 * LOG2_E)

    def head_block(classes, first_block):
        return pl.BlockSpec((1, heads_per_block, classes, block_m // classes, HEAD_DIM),
                            lambda i, j: (i // tiles_per_batch, jnp.maximum(j - first_block, 0), 0,
                                          i % tiles_per_batch, 0))

    def head_shape(classes, first_block):
        heads = in_width // HEAD_DIM - first_block * heads_per_block
        return jax.ShapeDtypeStruct((batch, heads, classes, seq // classes, HEAD_DIM), BF16)

    return pl.pallas_call(
        kernel,
        grid=(n_tok // block_m, in_width // block_n),
        in_specs=[
            pl.BlockSpec((block_m, d_model), lambda i, j: (i, 0)),
            _resident((1, d_model), lambda i, j: (0, 0)),
            pl.BlockSpec((d_model, block_n), lambda i, j: (0, j)),
        ],
        out_specs=[head_block(1, 1), head_block(CLASSES, 0)],
        out_shape=[head_shape(1, 1), head_shape(CLASSES, 0)],
        scratch_shapes=[pltpu.VMEM((block_m, d_model), BF16),
                        pltpu.VMEM((heads_per_block, block_m, HEAD_DIM), F32),
                        pltpu.VMEM((heads_per_block, 4, block_m // 4, HEAD_DIM), F32)],
        compiler_params=_params(("arbitrary", "arbitrary")),
        name="in_proj",
    )(x2, g, w_in)


def _dft_tables(seq, group_dim):
    part = seq // DFT_RADIX
    sub = CLASSES // DFT_RADIX
    s = np.arange(part, dtype=np.int64)[:, None]
    t_prime = np.array([sub * m + a for a in range(sub) for m in range(seq // CLASSES)],
                       dtype=np.int64)[None, :]
    tabs = []
    for r in range(DFT_RADIX):
        ang = 2.0 * np.pi * ((s * (DFT_RADIX * t_prime + r)) % seq).astype(np.float64) / seq
        tabs.append(np.concatenate([np.cos(ang), -np.sin(ang)], axis=0))
    seq_tab = np.stack(tabs).astype(np.float32)
    c = np.arange(group_dim, dtype=np.int64)
    ang = 2.0 * np.pi * ((c[:, None] * c[None, :]) % group_dim).astype(np.float64) / group_dim
    ortho = 1.0 / np.sqrt(float(seq) * float(group_dim))
    chan_tab = (np.concatenate([np.cos(ang), np.sin(ang)], axis=0) * ortho).astype(np.float32)
    return seq_tab, chan_tab


def _dft4(x0, x1, x2, x3):
    er, ei, fr, fi = x0[0] + x2[0], x0[1] + x2[1], x0[0] - x2[0], x0[1] - x2[1]
    gr, gi, hr, hi = x1[0] + x3[0], x1[1] + x3[1], x1[0] - x3[0], x1[1] - x3[1]
    return (er + gr, ei + gi), (fr + hi, fi - hr), (er - gr, ei - gi), (fr - hi, fi + hr)


def _dft8(x):
    half = 0.5 ** 0.5
    even = _dft4(*[(x[r][0] + x[r + 4][0], x[r][1] + x[r + 4][1]) for r in range(4)])
    d = [(x[r][0] - x[r + 4][0], x[r][1] - x[r + 4][1]) for r in range(4)]
    odd = _dft4(d[0],
                ((d[1][0] + d[1][1]) * half, (d[1][1] - d[1][0]) * half),
                (d[2][1], -d[2][0]),
                ((d[3][1] - d[3][0]) * half, (-d[3][0] - d[3][1]) * half))
    return [pair for pairs in zip(even, odd) for pair in pairs]


def _fourier_kernel(tab_ref, u_ref, chan_ref, wf_ref, y_ref):
    n_slabs, classes, members, _ = u_ref.shape
    groups, group_dim, _ = wf_ref.shape
    slabs = n_slabs // groups
    part = classes * members // DFT_RADIX
    def partial_dfts(grp):
        g = []
        for r in range(DFT_RADIX):
            u = jnp.concatenate(
                [jnp.concatenate([u_ref[grp * slabs + i, DFT_RADIX * a + r] for i in range(slabs)],
                                 axis=1)
                 for a in range(classes // DFT_RADIX)], axis=0)
            gr = jnp.dot(tab_ref[r], u, preferred_element_type=F32)
            g.append((gr[:part], gr[part:]))
        return g

    all_g = [partial_dfts(grp) for grp in range(groups)]
    for grp in range(groups):
        g = all_g[grp]
        chan_w = jnp.dot(chan_ref[...], wf_ref[grp], preferred_element_type=F32).astype(BF16)
        for q, (zr, zi) in enumerate(_dft8(g)):
            zc = jnp.concatenate([zr, zi], axis=1).astype(BF16)
            y = jnp.dot(zc, chan_w, preferred_element_type=F32)
            y_ref[0, q * part:(q + 1) * part, grp * group_dim:(grp + 1) * group_dim] = (
                y.astype(y_ref.dtype))


def _fourier_mix(heads_cls, w_fourier):
    batch, _, classes, members, _ = heads_cls.shape
    seq = classes * members
    part = seq // DFT_RADIX
    n_groups, group_dim, _ = w_fourier.shape
    fw = n_groups * group_dim
    slabs = group_dim // HEAD_DIM
    per_step = 2
    seq_tab, chan_tab = _dft_tables(seq, group_dim)
    seq_tab = jnp.asarray(seq_tab).astype(BF16)
    chan_tab = jnp.asarray(chan_tab).astype(BF16)

    return pl.pallas_call(
        _fourier_kernel,
        grid=(batch, n_groups // per_step),
        in_specs=[
            _resident((DFT_RADIX, 2 * part, part), lambda b, g: (0, 0, 0)),
            pl.BlockSpec((None, per_step * slabs, classes, members, HEAD_DIM),
                         lambda b, g: (b, g, 0, 0, 0)),
            _resident((2 * group_dim, group_dim), lambda b, g: (0, 0)),
            pl.BlockSpec((per_step, group_dim, group_dim), lambda b, g: (g, 0, 0)),
        ],
        out_specs=pl.BlockSpec((1, seq, per_step * group_dim), lambda b, g: (b, 0, g)),
        out_shape=jax.ShapeDtypeStruct((batch, seq, fw), BF16),
        compiler_params=_params(("arbitrary", "arbitrary")),
        name="fourier_mix",
    )(seq_tab, heads_cls, chan_tab, w_fourier)


def _attention_blocks(dil, seq):
    sub = CLASSES // dil
    q_rows, k_rows, radius = Q_BLOCK // sub, K_WINDOW // sub, RADIUS // sub
    members = seq // CLASSES
    for r in range(dil):
        for m0 in range(0, members, q_rows):
            w0 = min(max(m0 - radius, 0), members - k_rows)
            yield ((m0 - w0) // radius,
                   [(dil * a + r, m0, q_rows) for a in range(sub)],
                   [(dil * a + r, w0, k_rows) for a in range(sub)],
                   [(a * q_rows, q_rows, RESULT_PITCH * m0 + dil * a + r) for a in range(sub)])


def _attention_kernel(slopes_ref, q1_ref, k1_ref, v1_ref, qc_ref, kc_ref, vc_ref,
                      w0_ref, w1_ref, w2_ref,
                      o_ref, w0_out, w1_out, w2_out,
                      bias_scr, acc4_scr, m4_scr, den4_scr, acc16_scr, m16_scr, den16_scr):
    seq = o_ref.shape[0]
    slope = slopes_ref[pl.program_id(1)]

    for src, dst in ((w0_ref, w0_out), (w1_ref, w1_out), (w2_ref, w2_out)):
        tiles, _, width = dst.shape
        for t in range(tiles):
            dst[t] = src[:, t * width:(t + 1) * width].astype(dst.dtype)

    row = lax.broadcasted_iota(jnp.int32, (Q_BLOCK, K_WINDOW), 0)
    col = lax.broadcasted_iota(jnp.int32, (Q_BLOCK, K_WINDOW), 1)
    for gi, dil in enumerate(DILATIONS):
        sub = CLASSES // dil if dil > 1 else 1
        q_rows, k_rows, radius = Q_BLOCK // sub, K_WINDOW // sub, RADIUS // sub
        for variant in range(3):
            rel = sub * (col % k_rows - row % q_rows - variant * radius) + (col // k_rows - row // q_rows)
            rel = jnp.abs(rel)
            penalty = (slope * LOG2_E) * (dil * rel).astype(F32)
            bias_scr[gi * 3 + variant] = jnp.where(rel <= RADIUS, -penalty, MASK_VALUE)

    ones = jnp.ones((K_WINDOW, HEAD_DIM), BF16)

    def gather(ref, pieces):
        parts = [ref[c, r0:r0 + n, :] for c, r0, n in pieces]
        return parts[0] if len(parts) == 1 else jnp.concatenate(parts, axis=0)

    def softmax_block(gi, variant, q, k, v):
        s = lax.dot_general(q, k, (((1,), (1,)), ((), ())), preferred_element_type=F32)
        s = s + bias_scr[gi * 3 + variant]
        m = jnp.max(s, axis=1, keepdims=True)
        p = jnp.exp2(s - m).astype(BF16)
        acc = jnp.dot(p, jnp.concatenate([v, ones], axis=1), preferred_element_type=F32)
        return acc[:, :HEAD_DIM], jnp.broadcast_to(m, (Q_BLOCK, HEAD_DIM)), acc[:, HEAD_DIM:]

    def natural_rows(scr, qs):
        base = qs // CLASSES * RESULT_PITCH
        return jnp.concatenate([scr[base + i * RESULT_PITCH:base + i * RESULT_PITCH + CLASSES, :]
                                for i in range(Q_BLOCK // CLASSES)], axis=0)

    for gi, scratch in ((2, (acc16_scr, m16_scr, den16_scr)), (1, (acc4_scr, m4_scr, den4_scr))):
        for variant, q_pieces, k_pieces, out_pieces in _attention_blocks(DILATIONS[gi], seq):
            results = softmax_block(gi, variant, gather(qc_ref, q_pieces), gather(kc_ref, k_pieces),
                                    gather(vc_ref, k_pieces))
            for first, rows, start in out_pieces:
                for scr, value in zip(scratch, results):
                    scr[pl.ds(start, rows, stride=RESULT_PITCH), :] = value[first:first + rows]

    for qs in range(0, seq, Q_BLOCK):
        ws = min(max(qs - RADIUS, 0), seq - K_WINDOW)
        num, m, den = softmax_block(0, (qs - ws) // RADIUS, q1_ref[0, qs:qs + Q_BLOCK, :],
                                    k1_ref[0, ws:ws + K_WINDOW, :], v1_ref[0, ws:ws + K_WINDOW, :])
        m4, m16 = natural_rows(m4_scr, qs), natural_rows(m16_scr, qs)
        top = jnp.maximum(m, jnp.maximum(m4, m16))
        e1, e4, e16 = jnp.exp2(m - top), jnp.exp2(m4 - top), jnp.exp2(m16 - top)
        num = e1 * num + e4 * natural_rows(acc4_scr, qs) + e16 * natural_rows(acc16_scr, qs)
        den = e1 * den + e4 * natural_rows(den4_scr, qs) + e16 * natural_rows(den16_scr, qs)
        o_ref[qs:qs + Q_BLOCK, :] = (num * pl.reciprocal(den, approx=True)).astype(o_ref.dtype)


def _attention(heads_nat, heads_cls, n_heads, weights):
    batch, _, _, seq, _ = heads_nat.shape
    operands, in_specs = [], []
    for heads in (heads_nat, heads_cls):
        first = heads.shape[1] - 3 * n_heads
        for which in range(3):
            operands.append(heads)
            in_specs.append(pl.BlockSpec((None, None) + heads.shape[2:],
                                         lambda b, h, which=which, first=first:
                                         (b, first + which * n_heads + h, 0, 0, 0)))
    steps = batch * n_heads
    w_operands, w_in_specs, w_out_specs, w_out_shapes = [], [], [], []
    for w, width in weights:
        rows, cols = w.shape
        w_operands.append(w.reshape(steps, rows // steps, cols))
        w_in_specs.append(pl.BlockSpec((None, rows // steps, cols),
                                       lambda b, h: (b * n_heads + h, 0, 0)))
        w_out_specs.append(pl.BlockSpec((cols // width, rows // steps, width),
                                        lambda b, h: (0, b * n_heads + h, 0)))
        w_out_shapes.append(jax.ShapeDtypeStruct((cols // width, rows, width), BF16))
    slopes = jnp.asarray(2.0 ** (-8.0 * (np.arange(n_heads) + 1) / n_heads), dtype=F32)
    result = lambda: pltpu.VMEM((seq // CLASSES * RESULT_PITCH, HEAD_DIM), F32)
    return pl.pallas_call(
        _attention_kernel,
        grid=(batch, n_heads),
        in_specs=[pl.BlockSpec(memory_space=pltpu.SMEM)] + in_specs + w_in_specs,
        out_specs=[pl.BlockSpec((None, seq, HEAD_DIM), lambda b, h: (b, 0, h))] + w_out_specs,
        out_shape=[jax.ShapeDtypeStruct((batch, seq, n_heads * HEAD_DIM), BF16)] + w_out_shapes,
        scratch_shapes=[pltpu.VMEM((3 * len(DILATIONS), Q_BLOCK, K_WINDOW), F32)]
        + [result() for _ in range(6)],
        compiler_params=_params(("arbitrary", "arbitrary")),
        name="dilated_attention",
    )(slopes, *operands, *w_operands)


def _out_proj_kernel(yf_ref, ya_ref, x_ref, w_ref, h_ref):
    fw = yf_ref.shape[1]
    acc = jnp.dot(yf_ref[...], w_ref[:fw, :], preferred_element_type=F32)
    acc = acc + jnp.dot(ya_ref[...], w_ref[fw:, :], preferred_element_type=F32)
    h_ref[...] = x_ref[...] + acc


def _out_proj(yf, ya, x2, w_out, block_m):
    n_tok, d_model = x2.shape
    fw, aw = yf.shape[1], ya.shape[1]
    return pl.pallas_call(
        _out_proj_kernel,
        grid=(n_tok // block_m,),
        in_specs=[
            pl.BlockSpec((block_m, fw), lambda i: (i, 0)),
            pl.BlockSpec((block_m, aw), lambda i: (i, 0)),
            pl.BlockSpec((block_m, d_model), lambda i: (i, 0)),
            _resident((fw + aw, d_model), lambda i: (0, 0)),
        ],
        out_specs=pl.BlockSpec((block_m, d_model), lambda i: (i, 0)),
        out_shape=jax.ShapeDtypeStruct((n_tok, d_model), F32),
        compiler_params=_params(("arbitrary",)),
        name="out_proj",
    )(yf, ya, x2, w_out)


def _mlp_kernel(h_ref, g_ref, wu_ref, wd_ref, gf_ref, o_ref, u_scr, *, row_chunk):
    f = pl.program_id(1)
    last = pl.num_programs(1) - 1

    @pl.when(f == 0)
    def _():
        u_scr[...] = _rmsnorm_f32(h_ref[...], g_ref[...]).astype(BF16)

    def contribution(rows):
        a = jnp.dot(u_scr[rows, :], wu_ref[...], preferred_element_type=F32)
        a = jnp.square(jnp.maximum(a, 0.0)).astype(BF16)
        return jnp.dot(a, wd_ref[...], preferred_element_type=F32)

    chunks = [slice(c, c + row_chunk) for c in range(0, o_ref.shape[0], row_chunk)]

    @pl.when(f == 0)
    def _():
        for rows in chunks:
            o_ref[rows, :] = h_ref[rows, :] + contribution(rows)

    @pl.when(jnp.logical_and(f > 0, f < last))
    def _():
        for rows in chunks:
            o_ref[rows, :] += contribution(rows)

    @pl.when(f == last)
    def _():
        for rows in chunks:
            o_ref[rows, :] = _rmsnorm_f32(o_ref[rows, :] + contribution(rows), gf_ref[...])


def _mlp(h, g_mlp, w_up_tiles, w_down, g_final, block_m, row_chunk):
    n_tok, d_model = h.shape
    f_tiles, _, block_f = w_up_tiles.shape
    kernel = functools.partial(_mlp_kernel, row_chunk=row_chunk)
    return pl.pallas_call(
        kernel,
        grid=(n_tok // block_m, f_tiles),
        in_specs=[
            pl.BlockSpec((block_m, d_model), lambda i, f: (i, 0)),
            _resident((1, d_model), lambda i, f: (0, 0)),
            pl.BlockSpec((None, d_model, block_f), lambda i, f: (f, 0, 0)),
            pl.BlockSpec((block_f, d_model), lambda i, f: (f, 0)),
            _resident((1, d_model), lambda i, f: (0, 0)),
        ],
        out_specs=pl.BlockSpec((block_m, d_model), lambda i, f: (i, 0)),
        out_shape=jax.ShapeDtypeStruct((n_tok, d_model), F32),
        scratch_shapes=[pltpu.VMEM((block_m, d_model), BF16)],
        compiler_params=_params(("arbitrary", "arbitrary")),
        name="mlp",
    )(h, g_mlp, w_up_tiles, w_down, g_final)


def kernel(x, norm_mix_g, w_in, w_fourier, w_out, norm_mlp_g, w_up, w_down, norm_final_g):
    batch, seq, d_model = x.shape
    depth = w_in.shape[0]
    fourier_width = w_fourier.shape[1] * w_fourier.shape[2]
    n_heads = (w_in.shape[2] - fourier_width) // (3 * HEAD_DIM)
    d_ff = w_up.shape[2]
    assert w_in.shape[2] == 4 * fourier_width and n_heads * HEAD_DIM == fourier_width
    assert seq % (CLASSES * K_WINDOW) == 0
    assert depth == 1, "the MLP kernel fuses the final RMSNorm, so it handles a single layer"

    h = x.reshape(batch * seq, d_model)
    heads_nat, heads_cls = _in_proj(h, norm_mix_g[0][None, :], w_in[0].astype(BF16),
                                    batch, seq, block_m=1024)
    yf = _fourier_mix(heads_cls, w_fourier[0].astype(BF16))
    ya, w_up_tiles, w_down_b, w_out_b = _attention(
        heads_nat, heads_cls, n_heads,
        weights=((w_up[0], MLP_BLOCK_F), (w_down[0], d_model), (w_out[0], d_model)))
    h = _out_proj(yf.reshape(batch * seq, fourier_width),
                  ya.reshape(batch * seq, n_heads * HEAD_DIM),
                  h, w_out_b.reshape(d_model, d_model), block_m=512)
    h = _mlp(h, norm_mlp_g[0][None, :], w_up_tiles, w_down_b.reshape(d_ff, d_model),
             norm_final_g[None, :], block_m=1024, row_chunk=512)
    return h.reshape(batch, seq, d_model)
```

```python
import functools

import numpy as np
import jax
import jax.numpy as jnp
from jax import lax
from jax.experimental import pallas as pl
from jax.experimental.pallas import tpu as pltpu

F32 = jnp.float32
BF16 = jnp.bfloat16

HEAD_DIM = 128
WINDOW_DILATIONS = ((128, 1), (512, 4), (2048, 16))
DILATIONS = tuple(d for _, d in WINDOW_DILATIONS)
RADIUS = 64
assert all((w // 2) // d == RADIUS for w, d in WINDOW_DILATIONS)
RMS_EPS = 1e-6
MASK_VALUE = -1e30
LOG2_E = 1.4426950408889634

CLASSES = max(DILATIONS)
Q_BLOCK = 128
K_WINDOW = Q_BLOCK + 2 * RADIUS
DFT_RADIX = 8
MLP_BLOCK_F = 512
RESULT_PITCH = 24

VMEM_LIMIT_BYTES = 60 * 1024 * 1024


def _params(semantics):
    return pltpu.CompilerParams(dimension_semantics=semantics, vmem_limit_bytes=VMEM_LIMIT_BYTES)


def _resident(block_shape, index_map):
    return pl.BlockSpec(block_shape, index_map, pipeline_mode=pl.Buffered(1))


def _rmsnorm_f32(x, g):
    inv = lax.rsqrt(jnp.mean(x * x, axis=-1, keepdims=True) + RMS_EPS)
    return x * inv * g


def _in_proj_kernel(x_ref, g_ref, w_ref, nat_ref, cls_ref, u_scr, nat_scr, c4_scr, *,
                    heads_per_block, q_scale):
    j = pl.program_id(1)
    block_m = u_scr.shape[0]
    chunk_heads = 2
    chunk_n = chunk_heads * HEAD_DIM

    def project(scale):
        for c in range(heads_per_block // chunk_heads):
            acc = jnp.dot(u_scr[...], w_ref[:, c * chunk_n:(c + 1) * chunk_n],
                          preferred_element_type=F32)
            if scale is not None:
                acc = acc * scale
            for hh in range(c * chunk_heads, (c + 1) * chunk_heads):
                head = acc[:, (hh % chunk_heads) * HEAD_DIM:(hh % chunk_heads + 1) * HEAD_DIM]
                nat_ref[0, hh, 0] = head.astype(BF16)
                nat_scr[hh] = head
                for r in range(4):
                    c4_scr[hh, r] = nat_scr[hh, pl.ds(r, block_m // 4, stride=4), :]
                for r in range(4):
                    for a in range(4):
                        piece = c4_scr[hh, r, pl.ds(a, block_m // 16, stride=4), :]
                        cls_ref[0, hh, 4 * a + r] = piece.astype(BF16)

    @pl.when(j == 0)
    def _():
        u_scr[...] = _rmsnorm_f32(x_ref[...], g_ref[...]).astype(BF16)
        project(None)

    @pl.when(j == 1)
    def _():
        project(q_scale)

    @pl.when(j > 1)
    def _():
        project(None)


def _in_proj(x2, g, w_in, batch, seq, block_m):
    n_tok, d_model = x2.shape
    in_width = w_in.shape[1]
    block_n = in_width // 4
    heads_per_block = block_n // HEAD_DIM
    tiles_per_batch = seq // block_m
    kernel = functools.partial(_in_proj_kernel, heads_per_block=heads_per_block,
                               q_scale=HEAD_DIM ** -0.5 * LOG2_E)

    def head_block(classes, first_block):
        return pl.BlockSpec((1, heads_per_block, classes, block_m // classes, HEAD_DIM),
                            lambda i, j: (i // tiles_per_batch, jnp.maximum(j - first_block, 0), 0,
                                          i % tiles_per_batch, 0))

    def head_shape(classes, first_block):
        heads = in_width // HEAD_DIM - first_block * heads_per_block
        return jax.ShapeDtypeStruct((batch, heads, classes, seq // classes, HEAD_DIM), BF16)

    return pl.pallas_call(
        kernel,
        grid=(n_tok // block_m, in_width // block_n),
        in_specs=[
            pl.BlockSpec((block_m, d_model), lambda i, j: (i, 0)),
            _resident((1, d_model), lambda i, j: (0, 0)),
            pl.BlockSpec((d_model, block_n), lambda i, j: (0, j)),
        ],
        out_specs=[head_block(1, 1), head_block(CLASSES, 0)],
        out_shape=[head_shape(1, 1), head_shape(CLASSES, 0)],
        scratch_shapes=[pltpu.VMEM((block_m, d_model), BF16),
                        pltpu.VMEM((heads_per_block, block_m, HEAD_DIM), F32),
                        pltpu.VMEM((heads_per_block, 4, block_m // 4, HEAD_DIM), F32)],
        compiler_params=_params(("arbitrary", "arbitrary")),
        name="in_proj",
    )(x2, g, w_in)


def _dft_tables(seq, group_dim):
    part = seq // DFT_RADIX
    sub = CLASSES // DFT_RADIX
    s = np.arange(part, dtype=np.int64)[:, None]
    t_prime = np.array([sub * m + a for a in range(sub) for m in range(seq // CLASSES)],
                       dtype=np.int64)[None, :]
    tabs = []
    for r in range(DFT_RADIX):
        ang = 2.0 * np.pi * ((s * (DFT_RADIX * t_prime + r)) % seq).astype(np.float64) / seq
        tabs.append(np.concatenate([np.cos(ang), -np.sin(ang)], axis=0))
    seq_tab = np.stack(tabs).astype(np.float32)
    c = np.arange(group_dim, dtype=np.int64)
    ang = 2.0 * np.pi * ((c[:, None] * c[None, :]) % group_dim).astype(np.float64) / group_dim
    ortho = 1.0 / np.sqrt(float(seq) * float(group_dim))
    chan_tab = (np.concatenate([np.cos(ang), np.sin(ang)], axis=0) * ortho).astype(np.float32)
    return seq_tab, chan_tab


def _small_dft(x):
    n = len(x)
    if n == 1:
        return list(x)
    half = n // 2
    even = _small_dft([(x[r][0] + x[r + half][0], x[r][1] + x[r + half][1]) for r in range(half)])
    odd_in = []
    for r in range(half):
        dr, di = x[r][0] - x[r + half][0], x[r][1] - x[r + half][1]
        if r == 0:
            odd_in.append((dr, di))
        elif 4 * r == n:
            odd_in.append((di, -dr))
        elif 8 * r == n:
            odd_in.append(((dr + di) * 0.5 ** 0.5, (di - dr) * 0.5 ** 0.5))
        elif 8 * r == 3 * n:
            odd_in.append(((di - dr) * 0.5 ** 0.5, (-dr - di) * 0.5 ** 0.5))
        else:
            c, s = float(np.cos(2 * np.pi * r / n)), float(np.sin(2 * np.pi * r / n))
            odd_in.append((c * dr + s * di, c * di - s * dr))
    odd = _small_dft(odd_in)
    return [pair for pairs in zip(even, odd) for pair in pairs]


def _fourier_kernel(tab_ref, u_ref, chan_ref, wf_ref, y_ref):
    n_slabs, classes, members, _ = u_ref.shape
    groups, group_dim, _ = wf_ref.shape
    slabs = n_slabs // groups
    part = classes * members // DFT_RADIX

    def partial_dfts(grp):
        g = []
        for r in range(DFT_RADIX):
            u = jnp.concatenate(
                [jnp.concatenate([u_ref[grp * slabs + i, DFT_RADIX * a + r] for i in range(slabs)],
                                 axis=1)
                 for a in range(classes // DFT_RADIX)], axis=0)
            gr = jnp.dot(tab_ref[r], u, preferred_element_type=F32)
            g.append((gr[:part], gr[part:]))
        return g

    all_g = [partial_dfts(grp) for grp in range(groups)]
    for grp in range(groups):
        chan_w = jnp.dot(chan_ref[...], wf_ref[grp], preferred_element_type=F32).astype(BF16)
        z = jnp.concatenate([jnp.concatenate([zr, zi], axis=1).astype(BF16)
                             for zr, zi in _small_dft(all_g[grp])], axis=0)
        y = jnp.dot(z, chan_w, preferred_element_type=F32)
        y_ref[0, :, grp * group_dim:(grp + 1) * group_dim] = y.astype(y_ref.dtype)


def _fourier_mix(heads_cls, w_fourier):
    batch, _, classes, members, _ = heads_cls.shape
    seq = classes * members
    part = seq // DFT_RADIX
    n_groups, group_dim, _ = w_fourier.shape
    fw = n_groups * group_dim
    slabs = group_dim // HEAD_DIM
    per_step = 2
    seq_tab, chan_tab = _dft_tables(seq, group_dim)
    seq_tab = jnp.asarray(seq_tab).astype(BF16)
    chan_tab = jnp.asarray(chan_tab).astype(BF16)

    return pl.pallas_call(
        _fourier_kernel,
        grid=(batch, n_groups // per_step),
        in_specs=[
            _resident((DFT_RADIX, 2 * part, part), lambda b, g: (0, 0, 0)),
            pl.BlockSpec((None, per_step * slabs, classes, members, HEAD_DIM),
                         lambda b, g: (b, g, 0, 0, 0)),
            _resident((2 * group_dim, group_dim), lambda b, g: (0, 0)),
            pl.BlockSpec((per_step, group_dim, group_dim), lambda b, g: (g, 0, 0)),
        ],
        out_specs=pl.BlockSpec((1, seq, per_step * group_dim), lambda b, g: (b, 0, g)),
        out_shape=jax.ShapeDtypeStruct((batch, seq, fw), BF16),
        compiler_params=_params(("arbitrary", "arbitrary")),
        name="fourier_mix",
    )(seq_tab, heads_cls, chan_tab, w_fourier)


def _attention_blocks(dil, seq):
    sub = CLASSES // dil
    q_rows, k_rows, radius = Q_BLOCK // sub, K_WINDOW // sub, RADIUS // sub
    members = seq // CLASSES
    for r in range(dil):
        for m0 in range(0, members, q_rows):
            w0 = min(max(m0 - radius, 0), members - k_rows)
            yield ((m0 - w0) // radius,
                   [(dil * a + r, m0, q_rows) for a in range(sub)],
                   [(dil * a + r, w0, k_rows) for a in range(sub)],
                   [(a * q_rows, q_rows, RESULT_PITCH * m0 + dil * a + r) for a in range(sub)])


def _attention_kernel(slopes_ref, q1_ref, k1_ref, v1_ref, qc_ref, kc_ref, vc_ref,
                      w0_ref, w1_ref, w2_ref,
                      o_ref, w0_out, w1_out, w2_out,
                      bias_scr, acc4_scr, m4_scr, den4_scr, acc16_scr, m16_scr, den16_scr):
    seq = o_ref.shape[0]
    slope = slopes_ref[pl.program_id(1)]

    for src, dst in ((w0_ref, w0_out), (w1_ref, w1_out), (w2_ref, w2_out)):
        tiles, _, width = dst.shape
        for t in range(tiles):
            dst[t] = src[:, t * width:(t + 1) * width].astype(dst.dtype)

    row = lax.broadcasted_iota(jnp.int32, (Q_BLOCK, K_WINDOW), 0)
    col = lax.broadcasted_iota(jnp.int32, (Q_BLOCK, K_WINDOW), 1)
    for gi, dil in enumerate(DILATIONS):
        sub = CLASSES // dil if dil > 1 else 1
        q_rows, k_rows, radius = Q_BLOCK // sub, K_WINDOW // sub, RADIUS // sub
        for variant in range(3):
            rel = sub * (col % k_rows - row % q_rows - variant * radius) + (col // k_rows - row // q_rows)
            rel = jnp.abs(rel)
            penalty = (slope * LOG2_E) * (dil * rel).astype(F32)
            bias_scr[gi * 3 + variant] = jnp.where(rel <= RADIUS, -penalty, MASK_VALUE)

    ones = jnp.ones((K_WINDOW, HEAD_DIM), BF16)

    def gather(ref, pieces):
        parts = [ref[c, r0:r0 + n, :] for c, r0, n in pieces]
        return parts[0] if len(parts) == 1 else jnp.concatenate(parts, axis=0)

    def softmax_block(gi, variant, q, k, v):
        s = lax.dot_general(q, k, (((1,), (1,)), ((), ())), preferred_element_type=F32)
        s = s + bias_scr[gi * 3 + variant]
        m = jnp.max(s, axis=1, keepdims=True)
        p = jnp.exp2(s - m).astype(BF16)
        acc = jnp.dot(p, jnp.concatenate([v, ones], axis=1), preferred_element_type=F32)
        return acc[:, :HEAD_DIM], jnp.broadcast_to(m, (Q_BLOCK, HEAD_DIM)), acc[:, HEAD_DIM:]

    def natural_rows(scr, qs):
        base = qs // CLASSES * RESULT_PITCH
        return jnp.concatenate([scr[base + i * RESULT_PITCH:base + i * RESULT_PITCH + CLASSES, :]
                                for i in range(Q_BLOCK // CLASSES)], axis=0)

    for gi, scratch in ((2, (acc16_scr, m16_scr, den16_scr)), (1, (acc4_scr, m4_scr, den4_scr))):
        for variant, q_pieces, k_pieces, out_pieces in _attention_blocks(DILATIONS[gi], seq):
            results = softmax_block(gi, variant, gather(qc_ref, q_pieces), gather(kc_ref, k_pieces),
                                    gather(vc_ref, k_pieces))
            for first, rows, start in out_pieces:
                for scr, value in zip(scratch, results):
                    scr[pl.ds(start, rows, stride=RESULT_PITCH), :] = value[first:first + rows]

    for qs in range(0, seq, Q_BLOCK):
        ws = min(max(qs - RADIUS, 0), seq - K_WINDOW)
        num, m, den = softmax_block(0, (qs - ws) // RADIUS, q1_ref[0, qs:qs + Q_BLOCK, :],
                                    k1_ref[0, ws:ws + K_WINDOW, :], v1_ref[0, ws:ws + K_WINDOW, :])
        m4, m16 = natural_rows(m4_scr, qs), natural_rows(m16_scr, qs)
        top = jnp.maximum(m, jnp.maximum(m4, m16))
        e1, e4, e16 = jnp.exp2(m - top), jnp.exp2(m4 - top), jnp.exp2(m16 - top)
        num = e1 * num + e4 * natural_rows(acc4_scr, qs) + e16 * natural_rows(acc16_scr, qs)
        den = e1 * den + e4 * natural_rows(den4_scr, qs) + e16 * natural_rows(den16_scr, qs)
        o_ref[qs:qs + Q_BLOCK, :] = (num * pl.reciprocal(den, approx=True)).astype(o_ref.dtype)


def _attention(heads_nat, heads_cls, n_heads, weights):
    batch, _, _, seq, _ = heads_nat.shape
    operands, in_specs = [], []
    for heads in (heads_nat, heads_cls):
        first = heads.shape[1] - 3 * n_heads
        for which in range(3):
            operands.append(heads)
            in_specs.append(pl.BlockSpec((None, None) + heads.shape[2:],
                                         lambda b, h, which=which, first=first:
                                         (b, first + which * n_heads + h, 0, 0, 0)))
    steps = batch * n_heads
    w_operands, w_in_specs, w_out_specs, w_out_shapes = [], [], [], []
    for w, width in weights:
        rows, cols = w.shape
        w_operands.append(w.reshape(steps, rows // steps, cols))
        w_in_specs.append(pl.BlockSpec((None, rows // steps, cols),
                                       lambda b, h: (b * n_heads + h, 0, 0)))
        w_out_specs.append(pl.BlockSpec((cols // width, rows // steps, width),
                                        lambda b, h: (0, b * n_heads + h, 0)))
        w_out_shapes.append(jax.ShapeDtypeStruct((cols // width, rows, width), BF16))
    slopes = jnp.asarray(2.0 ** (-8.0 * (np.arange(n_heads) + 1) / n_heads), dtype=F32)
    result = lambda: pltpu.VMEM((seq // CLASSES * RESULT_PITCH, HEAD_DIM), F32)
    return pl.pallas_call(
        _attention_kernel,
        grid=(batch, n_heads),
        in_specs=[pl.BlockSpec(memory_space=pltpu.SMEM)] + in_specs + w_in_specs,
        out_specs=[pl.BlockSpec((None, seq, HEAD_DIM), lambda b, h: (b, 0, h))] + w_out_specs,
        out_shape=[jax.ShapeDtypeStruct((batch, seq, n_heads * HEAD_DIM), BF16)] + w_out_shapes,
        scratch_shapes=[pltpu.VMEM((3 * len(DILATIONS), Q_BLOCK, K_WINDOW), F32)]
        + [result() for _ in range(6)],
        compiler_params=_params(("arbitrary", "arbitrary")),
        name="dilated_attention",
    )(slopes, *operands, *w_operands)


def _out_proj_kernel(yf_ref, ya_ref, x_ref, w_ref, h_ref):
    fw = yf_ref.shape[1]
    acc = jnp.dot(yf_ref[...], w_ref[:fw, :], preferred_element_type=F32)
    acc = acc + jnp.dot(ya_ref[...], w_ref[fw:, :], preferred_element_type=F32)
    h_ref[...] = x_ref[...] + acc


def _out_proj(yf, ya, x2, w_out, block_m):
    n_tok, d_model = x2.shape
    fw, aw = yf.shape[1], ya.shape[1]
    return pl.pallas_call(
        _out_proj_kernel,
        grid=(n_tok // block_m,),
        in_specs=[
            pl.BlockSpec((block_m, fw), lambda i: (i, 0)),
            pl.BlockSpec((block_m, aw), lambda i: (i, 0)),
            pl.BlockSpec((block_m, d_model), lambda i: (i, 0)),
            _resident((fw + aw, d_model), lambda i: (0, 0)),
        ],
        out_specs=pl.BlockSpec((block_m, d_model), lambda i: (i, 0)),
        out_shape=jax.ShapeDtypeStruct((n_tok, d_model), F32),
        compiler_params=_params(("arbitrary",)),
        name="out_proj",
    )(yf, ya, x2, w_out)


def _mlp_kernel(h_ref, g_ref, wu_ref, wd_ref, gf_ref, o_ref, u_scr, *, row_chunk):
    f = pl.program_id(1)
    last = pl.num_programs(1) - 1

    @pl.when(f == 0)
    def _():
        u_scr[...] = _rmsnorm_f32(h_ref[...], g_ref[...]).astype(BF16)

    def contribution(rows):
        a = jnp.dot(u_scr[rows, :], wu_ref[...], preferred_element_type=F32)
        a = jnp.square(jnp.maximum(a, 0.0)).astype(BF16)
        return jnp.dot(a, wd_ref[...], preferred_element_type=F32)

    chunks = [slice(c, c + row_chunk) for c in range(0, o_ref.shape[0], row_chunk)]

    @pl.when(f == 0)
    def _():
        for rows in chunks:
            o_ref[rows, :] = h_ref[rows, :] + contribution(rows)

    @pl.when(jnp.logical_and(f > 0, f < last))
    def _():
        for rows in chunks:
            o_ref[rows, :] += contribution(rows)

    @pl.when(f == last)
    def _():
        for rows in chunks:
            o_ref[rows, :] = _rmsnorm_f32(o_ref[rows, :] + contribution(rows), gf_ref[...])


def _mlp(h, g_mlp, w_up_tiles, w_down, g_final, block_m, row_chunk):
    n_tok, d_model = h.shape
    f_tiles, _, block_f = w_up_tiles.shape
    kernel = functools.partial(_mlp_kernel, row_chunk=row_chunk)
    return pl.pallas_call(
        kernel,
        grid=(n_tok // block_m, f_tiles),
        in_specs=[
            pl.BlockSpec((block_m, d_model), lambda i, f: (i, 0)),
            _resident((1, d_model), lambda i, f: (0, 0)),
            pl.BlockSpec((None, d_model, block_f), lambda i, f: (f, 0, 0)),
            pl.BlockSpec((block_f, d_model), lambda i, f: (f, 0)),
            _resident((1, d_model), lambda i, f: (0, 0)),
        ],
        out_specs=pl.BlockSpec((block_m, d_model), lambda i, f: (i, 0)),
        out_shape=jax.ShapeDtypeStruct((n_tok, d_model), F32),
        scratch_shapes=[pltpu.VMEM((block_m, d_model), BF16)],
        compiler_params=_params(("arbitrary", "arbitrary")),
        name="mlp",
    )(h, g_mlp, w_up_tiles, w_down, g_final)


def kernel(x, norm_mix_g, w_in, w_fourier, w_out, norm_mlp_g, w_up, w_down, norm_final_g):
    batch, seq, d_model = x.shape
    depth = w_in.shape[0]
    fourier_width = w_fourier.shape[1] * w_fourier.shape[2]
    n_heads = (w_in.shape[2] - fourier_width) // (3 * HEAD_DIM)
    d_ff = w_up.shape[2]
    assert w_in.shape[2] == 4 * fourier_width and n_heads * HEAD_DIM == fourier_width
    assert seq % (CLASSES * K_WINDOW) == 0
    assert depth == 1, "the MLP kernel fuses the final RMSNorm, so it handles a single layer"

    h = x.reshape(batch * seq, d_model)
    heads_nat, heads_cls = _in_proj(h, norm_mix_g[0][None, :], w_in[0].astype(BF16),
                                    batch, seq, block_m=1024)
    yf = _fourier_mix(heads_cls, w_fourier[0].astype(BF16))
    ya, w_up_tiles, w_down_b, w_out_b = _attention(
        heads_nat, heads_cls, n_heads,
        weights=((w_up[0], MLP_BLOCK_F), (w_down[0], d_model), (w_out[0], d_model)))
    h = _out_proj(yf.reshape(batch * seq, fourier_width),
                  ya.reshape(batch * seq, n_heads * HEAD_DIM),
                  h, w_out_b.reshape(d_model, d_model), block_m=1024)
    h = _mlp(h, norm_mlp_g[0][None, :], w_up_tiles, w_down_b.reshape(d_ff, d_model),
             norm_final_g[None, :], block_m=1024, row_chunk=512)
    return h.reshape(batch, seq, d_model)
```

```python
import functools

import numpy as np
import jax
import jax.numpy as jnp
from jax import lax
from jax.experimental import pallas as pl
from jax.experimental.pallas import tpu as pltpu

F32 = jnp.float32
BF16 = jnp.bfloat16

HEAD_DIM = 128
WINDOW_DILATIONS = ((128, 1), (512, 4), (2048, 16))
DILATIONS = tuple(d for _, d in WINDOW_DILATIONS)
RADIUS = 64
assert all((w // 2) // d == RADIUS for w, d in WINDOW_DILATIONS)
RMS_EPS = 1e-6
MASK_VALUE = -1e30
LOG2_E = 1.4426950408889634

CLASSES = max(DILATIONS)
Q_BLOCK = 128
K_WINDOW = Q_BLOCK + 2 * RADIUS
DFT_RADIX = 8
MLP_BLOCK_F = 512
RESULT_PITCH = 24

VMEM_LIMIT_BYTES = 60 * 1024 * 1024


def _params(semantics):
    return pltpu.CompilerParams(dimension_semantics=semantics, vmem_limit_bytes=VMEM_LIMIT_BYTES)


def _resident(block_shape, index_map):
    return pl.BlockSpec(block_shape, index_map, pipeline_mode=pl.Buffered(1))


def _rmsnorm_f32(x, g):
    inv = lax.rsqrt(jnp.mean(x * x, axis=-1, keepdims=True) + RMS_EPS)
    return x * inv * g


def _in_proj_kernel(x_ref, g_ref, w_ref, nat_ref, cls_ref, u_scr, nat_scr, c4_scr, *,
                    heads_per_block, q_scale):
    j = pl.program_id(1)
    block_m = u_scr.shape[0]
    chunk_heads = 2
    chunk_n = chunk_heads * HEAD_DIM

    def project(scale):
        for c in range(heads_per_block // chunk_heads):
            acc = jnp.dot(u_scr[...], w_ref[:, c * chunk_n:(c + 1) * chunk_n],
                          preferred_element_type=F32)
            if scale is not None:
                acc = acc * scale
            for hh in range(c * chunk_heads, (c + 1) * chunk_heads):
                head = acc[:, (hh % chunk_heads) * HEAD_DIM:(hh % chunk_heads + 1) * HEAD_DIM]
                nat_ref[0, hh, 0] = head.astype(BF16)
                nat_scr[hh] = head
                for r in range(4):
                    c4_scr[hh, r] = nat_scr[hh, pl.ds(r, block_m // 4, stride=4), :]
                for r in range(4):
                    for a in range(4):
                        piece = c4_scr[hh, r, pl.ds(a, block_m // 16, stride=4), :]
                        cls_ref[0, hh, 4 * a + r] = piece.astype(BF16)

    @pl.when(j == 0)
    def _():
        u_scr[...] = _rmsnorm_f32(x_ref[...], g_ref[...]).astype(BF16)
        project(None)

    @pl.when(j == 1)
    def _():
        project(q_scale)

    @pl.when(j > 1)
    def _():
        project(None)


def _in_proj(x2, g, w_in, batch, seq, block_m):
    n_tok, d_model = x2.shape
    in_width = w_in.shape[1]
    block_n = in_width // 4
    heads_per_block = block_n // HEAD_DIM
    tiles_per_batch = seq // block_m
    kernel = functools.partial(_in_proj_kernel, heads_per_block=heads_per_block,
                               q_scale=HEAD_DIM ** -0.5 * LOG2_E)

    def head_block(classes, first_block):
        return pl.BlockSpec((1, heads_per_block, classes, block_m // classes, HEAD_DIM),
                            lambda i, j: (i // tiles_per_batch, jnp.maximum(j - first_block, 0), 0,
                                          i % tiles_per_batch, 0))

    def head_shape(classes, first_block):
        heads = in_width // HEAD_DIM - first_block * heads_per_block
        return jax.ShapeDtypeStruct((batch, heads, classes, seq // classes, HEAD_DIM), BF16)

    return pl.pallas_call(
        kernel,
        grid=(n_tok // block_m, in_width // block_n),
        in_specs=[
            pl.BlockSpec((block_m, d_model), lambda i, j: (i, 0)),
            _resident((1, d_model), lambda i, j: (0, 0)),
            pl.BlockSpec((d_model, block_n), lambda i, j: (0, j)),
        ],
        out_specs=[head_block(1, 1), head_block(CLASSES, 0)],
        out_shape=[head_shape(1, 1), head_shape(CLASSES, 0)],
        scratch_shapes=[pltpu.VMEM((block_m, d_model), BF16),
                        pltpu.VMEM((heads_per_block, block_m, HEAD_DIM), F32),
                        pltpu.VMEM((heads_per_block, 4, block_m // 4, HEAD_DIM), F32)],
        compiler_params=_params(("arbitrary", "arbitrary")),
        name="in_proj",
    )(x2, g, w_in)


def _dft_tables(seq, group_dim):
    part = seq // DFT_RADIX
    sub = CLASSES // DFT_RADIX
    s = np.arange(part, dtype=np.int64)[:, None]
    t_prime = np.array([sub * m + a for a in range(sub) for m in range(seq // CLASSES)],
                       dtype=np.int64)[None, :]
    tabs = []
    for r in range(DFT_RADIX):
        ang = 2.0 * np.pi * ((s * (DFT_RADIX * t_prime + r)) % seq).astype(np.float64) / seq
        tabs.append(np.concatenate([np.cos(ang), -np.sin(ang)], axis=0))
    seq_tab = np.stack(tabs).astype(np.float32)
    c = np.arange(group_dim, dtype=np.int64)
    ang = 2.0 * np.pi * ((c[:, None] * c[None, :]) % group_dim).astype(np.float64) / group_dim
    ortho = 1.0 / np.sqrt(float(seq) * float(group_dim))
    chan_tab = (np.concatenate([np.cos(ang), np.sin(ang)], axis=0) * ortho).astype(np.float32)
    return seq_tab, chan_tab


def _small_dft(x):
    n = len(x)
    if n == 1:
        return list(x)
    half = n // 2
    even = _small_dft([(x[r][0] + x[r + half][0], x[r][1] + x[r + half][1]) for r in range(half)])
    odd_in = []
    for r in range(half):
        dr, di = x[r][0] - x[r + half][0], x[r][1] - x[r + half][1]
        if r == 0:
            odd_in.append((dr, di))
        elif 4 * r == n:
            odd_in.append((di, -dr))
        elif 8 * r == n:
            odd_in.append(((dr + di) * 0.5 ** 0.5, (di - dr) * 0.5 ** 0.5))
        elif 8 * r == 3 * n:
            odd_in.append(((di - dr) * 0.5 ** 0.5, (-dr - di) * 0.5 ** 0.5))
        else:
            c, s = float(np.cos(2 * np.pi * r / n)), float(np.sin(2 * np.pi * r / n))
            odd_in.append((c * dr + s * di, c * di - s * dr))
    odd = _small_dft(odd_in)
    return [pair for pairs in zip(even, odd) for pair in pairs]


def _fourier_kernel(tab_ref, u_ref, chan_ref, wf_ref, y_ref):
    n_slabs, classes, members, _ = u_ref.shape
    groups, group_dim, _ = wf_ref.shape
    slabs = n_slabs // groups
    part = classes * members // DFT_RADIX

    def partial_dfts(grp):
        g = []
        for r in range(DFT_RADIX):
            u = jnp.concatenate(
                [jnp.concatenate([u_ref[grp * slabs + i, DFT_RADIX * a + r] for i in range(slabs)],
                                 axis=1)
                 for a in range(classes // DFT_RADIX)], axis=0)
            gr = jnp.dot(tab_ref[r], u, preferred_element_type=F32)
            g.append((gr[:part], gr[part:]))
        return g

    all_g = [partial_dfts(grp) for grp in range(groups)]
    for grp in range(groups):
        chan_w = jnp.dot(chan_ref[...], wf_ref[grp], preferred_element_type=F32).astype(BF16)
        z = jnp.concatenate([jnp.concatenate([zr, zi], axis=1).astype(BF16)
                             for zr, zi in _small_dft(all_g[grp])], axis=0)
        y = jnp.dot(z, chan_w, preferred_element_type=F32)
        y_ref[0, :, grp * group_dim:(grp + 1) * group_dim] = y.astype(y_ref.dtype)


def _fourier_mix(heads_cls, w_fourier):
    batch, _, classes, members, _ = heads_cls.shape
    seq = classes * members
    part = seq // DFT_RADIX
    n_groups, group_dim, _ = w_fourier.shape
    fw = n_groups * group_dim
    slabs = group_dim // HEAD_DIM
    per_step = 2
    seq_tab, chan_tab = _dft_tables(seq, group_dim)
    seq_tab = jnp.asarray(seq_tab).astype(BF16)
    chan_tab = jnp.asarray(chan_tab).astype(BF16)

    return pl.pallas_call(
        _fourier_kernel,
        grid=(batch, n_groups // per_step),
        in_specs=[
            _resident((DFT_RADIX, 2 * part, part), lambda b, g: (0, 0, 0)),
            pl.BlockSpec((None, per_step * slabs, classes, members, HEAD_DIM),
                         lambda b, g: (b, g, 0, 0, 0)),
            _resident((2 * group_dim, group_dim), lambda b, g: (0, 0)),
            pl.BlockSpec((per_step, group_dim, group_dim), lambda b, g: (g, 0, 0)),
        ],
        out_specs=pl.BlockSpec((1, seq, per_step * group_dim), lambda b, g: (b, 0, g)),
        out_shape=jax.ShapeDtypeStruct((batch, seq, fw), BF16),
        compiler_params=_params(("arbitrary", "arbitrary")),
        name="fourier_mix",
    )(seq_tab, heads_cls, chan_tab, w_fourier)


def _attention_blocks(dil, seq):
    sub = CLASSES // dil
    q_rows, k_rows, radius = Q_BLOCK // sub, K_WINDOW // sub, RADIUS // sub
    members = seq // CLASSES
    for r in range(dil):
        for m0 in range(0, members, q_rows):
            w0 = min(max(m0 - radius, 0), members - k_rows)
            yield ((m0 - w0) // radius,
                   [(dil * a + r, m0, q_rows) for a in range(sub)],
                   [(dil * a + r, w0, k_rows) for a in range(sub)],
                   [(a * q_rows, q_rows, RESULT_PITCH * m0 + dil * a + r) for a in range(sub)])


def _attention_kernel(slopes_ref, q1_ref, k1_ref, v1_ref, qc_ref, kc_ref, vc_ref,
                      w0_ref, w1_ref, w2_ref,
                      o_ref, w0_out, w1_out, w2_out,
                      bias_scr, acc4_scr, m4_scr, den4_scr, acc16_scr, m16_scr, den16_scr):
    seq = o_ref.shape[0]
    slope = slopes_ref[pl.program_id(1)]

    for src, dst in ((w0_ref, w0_out), (w1_ref, w1_out), (w2_ref, w2_out)):
        tiles, _, width = dst.shape
        for t in range(tiles):
            dst[t] = src[:, t * width:(t + 1) * width].astype(dst.dtype)

    row = lax.broadcasted_iota(jnp.int32, (Q_BLOCK, K_WINDOW), 0)
    col = lax.broadcasted_iota(jnp.int32, (Q_BLOCK, K_WINDOW), 1)
    for gi, dil in enumerate(DILATIONS):
        sub = CLASSES // dil if dil > 1 else 1
        q_rows, k_rows, radius = Q_BLOCK // sub, K_WINDOW // sub, RADIUS // sub
        for variant in range(3):
            rel = sub * (col % k_rows - row % q_rows - variant * radius) + (col // k_rows - row // q_rows)
            rel = jnp.abs(rel)
            penalty = (slope * LOG2_E) * (dil * rel).astype(F32)
            bias_scr[gi * 3 + variant] = jnp.where(rel <= RADIUS, -penalty, MASK_VALUE)

    ones = jnp.ones((K_WINDOW, HEAD_DIM), BF16)

    def gather(ref, pieces):
        parts = [ref[c, r0:r0 + n, :] for c, r0, n in pieces]
        return parts[0] if len(parts) == 1 else jnp.concatenate(parts, axis=0)

    def softmax_block(gi, variant, q, k, v):
        s = lax.dot_general(q, k, (((1,), (1,)), ((), ())), preferred_element_type=F32)
        s = s + bias_scr[gi * 3 + variant]
        m = jnp.max(s, axis=1, keepdims=True)
        p = jnp.exp2(s - m).astype(BF16)
        acc = jnp.dot(p, jnp.concatenate([v, ones], axis=1), preferred_element_type=F32)
        return acc[:, :HEAD_DIM], jnp.broadcast_to(m, (Q_BLOCK, HEAD_DIM)), acc[:, HEAD_DIM:]

    def natural_rows(scr, qs):
        base = qs // CLASSES * RESULT_PITCH
        return jnp.concatenate([scr[base + i * RESULT_PITCH:base + i * RESULT_PITCH + CLASSES, :]
                                for i in range(Q_BLOCK // CLASSES)], axis=0)

    for gi, scratch in ((2, (acc16_scr, m16_scr, den16_scr)), (1, (acc4_scr, m4_scr, den4_scr))):
        for variant, q_pieces, k_pieces, out_pieces in _attention_blocks(DILATIONS[gi], seq):
            results = softmax_block(gi, variant, gather(qc_ref, q_pieces), gather(kc_ref, k_pieces),
                                    gather(vc_ref, k_pieces))
            for first, rows, start in out_pieces:
                for scr, value in zip(scratch, results):
                    scr[pl.ds(start, rows, stride=RESULT_PITCH), :] = value[first:first + rows]

    for qs in range(0, seq, Q_BLOCK):
        ws = min(max(qs - RADIUS, 0), seq - K_WINDOW)
        num, m, den = softmax_block(0, (qs - ws) // RADIUS, q1_ref[0, qs:qs + Q_BLOCK, :],
                                    k1_ref[0, ws:ws + K_WINDOW, :], v1_ref[0, ws:ws + K_WINDOW, :])
        m4, m16 = natural_rows(m4_scr, qs), natural_rows(m16_scr, qs)
        top = jnp.maximum(m, jnp.maximum(m4, m16))
        e1, e4, e16 = jnp.exp2(m - top), jnp.exp2(m4 - top), jnp.exp2(m16 - top)
        num = e1 * num + e4 * natural_rows(acc4_scr, qs) + e16 * natural_rows(acc16_scr, qs)
        den = e1 * den + e4 * natural_rows(den4_scr, qs) + e16 * natural_rows(den16_scr, qs)
        o_ref[qs:qs + Q_BLOCK, :] = (num * pl.reciprocal(den, approx=True)).astype(o_ref.dtype)


def _attention(heads_nat, heads_cls, n_heads, weights):
    batch, _, _, seq, _ = heads_nat.shape
    operands, in_specs = [], []
    for heads in (heads_nat, heads_cls):
        first = heads.shape[1] - 3 * n_heads
        for which in range(3):
            operands.append(heads)
            in_specs.append(pl.BlockSpec((None, None) + heads.shape[2:],
                                         lambda b, h, which=which, first=first:
                                         (b, first + which * n_heads + h, 0, 0, 0)))
    steps = batch * n_heads
    w_operands, w_in_specs, w_out_specs, w_out_shapes = [], [], [], []
    for w, width in weights:
        rows, cols = w.shape
        w_operands.append(w.reshape(steps, rows // steps, cols))
        w_in_specs.append(pl.BlockSpec((None, rows // steps, cols),
                                       lambda b, h: (b * n_heads + h, 0, 0)))
        w_out_specs.append(pl.BlockSpec((cols // width, rows // steps, width),
                                        lambda b, h: (0, b * n_heads + h, 0)))
        w_out_shapes.append(jax.ShapeDtypeStruct((cols // width, rows, width), BF16))
    slopes = jnp.asarray(2.0 ** (-8.0 * (np.arange(n_heads) + 1) / n_heads), dtype=F32)
    result = lambda: pltpu.VMEM((seq // CLASSES * RESULT_PITCH, HEAD_DIM), F32)
    return pl.pallas_call(
        _attention_kernel,
        grid=(batch, n_heads),
        in_specs=[pl.BlockSpec(memory_space=pltpu.SMEM)] + in_specs + w_in_specs,
        out_specs=[pl.BlockSpec((None, seq, HEAD_DIM), lambda b, h: (b, 0, h))] + w_out_specs,
        out_shape=[jax.ShapeDtypeStruct((batch, seq, n_heads * HEAD_DIM), BF16)] + w_out_shapes,
        scratch_shapes=[pltpu.VMEM((3 * len(DILATIONS), Q_BLOCK, K_WINDOW), F32)]
        + [result() for _ in range(6)],
        compiler_params=_params(("arbitrary", "arbitrary")),
        name="dilated_attention",
    )(slopes, *operands, *w_operands)


def _out_proj_kernel(yf_ref, ya_ref, x_ref, w_ref, h_ref):
    fw = yf_ref.shape[1]
    acc = jnp.dot(yf_ref[...], w_ref[:fw, :], preferred_element_type=F32)
    acc = acc + jnp.dot(ya_ref[...], w_ref[fw:, :], preferred_element_type=F32)
    h_ref[...] = x_ref[...] + acc


def _out_proj(yf, ya, x2, w_out, block_m):
    n_tok, d_model = x2.shape
    fw, aw = yf.shape[1], ya.shape[1]
    return pl.pallas_call(
        _out_proj_kernel,
        grid=(n_tok // block_m,),
        in_specs=[
            pl.BlockSpec((block_m, fw), lambda i: (i, 0)),
            pl.BlockSpec((block_m, aw), lambda i: (i, 0)),
            pl.BlockSpec((block_m, d_model), lambda i: (i, 0)),
            _resident((fw + aw, d_model), lambda i: (0, 0)),
        ],
        out_specs=pl.BlockSpec((block_m, d_model), lambda i: (i, 0)),
        out_shape=jax.ShapeDtypeStruct((n_tok, d_model), F32),
        compiler_params=_params(("arbitrary",)),
        name="out_proj",
    )(yf, ya, x2, w_out)


def _mlp_kernel(h_ref, g_ref, wu_ref, wd_ref, gf_ref, o_ref, u_scr, *, row_chunk):
    f = pl.program_id(1)
    last = pl.num_programs(1) - 1

    @pl.when(f == 0)
    def _():
        u_scr[...] = _rmsnorm_f32(h_ref[...], g_ref[...]).astype(BF16)

    def contribution(rows):
        a = jnp.dot(u_scr[rows, :], wu_ref[...], preferred_element_type=F32)
        a = jnp.square(jnp.maximum(a, 0.0)).astype(BF16)
        return jnp.dot(a, wd_ref[...], preferred_element_type=F32)

    chunks = [slice(c, c + row_chunk) for c in range(0, o_ref.shape[0], row_chunk)]

    @pl.when(f == 0)
    def _():
        for rows in chunks:
            o_ref[rows, :] = h_ref[rows, :] + contribution(rows)

    @pl.when(jnp.logical_and(f > 0, f < last))
    def _():
        for rows in chunks:
            o_ref[rows, :] += contribution(rows)

    @pl.when(f == last)
    def _():
        for rows in chunks:
            o_ref[rows, :] = _rmsnorm_f32(o_ref[rows, :] + contribution(rows), gf_ref[...])


def _mlp(h, g_mlp, w_up_tiles, w_down, g_final, block_m, row_chunk):
    n_tok, d_model = h.shape
    f_tiles, _, block_f = w_up_tiles.shape
    kernel = functools.partial(_mlp_kernel, row_chunk=row_chunk)
    return pl.pallas_call(
        kernel,
        grid=(n_tok // block_m, f_tiles),
        in_specs=[
            pl.BlockSpec((block_m, d_model), lambda i, f: (i, 0)),
            _resident((1, d_model), lambda i, f: (0, 0)),
            pl.BlockSpec((None, d_model, block_f), lambda i, f: (f, 0, 0)),
            pl.BlockSpec((block_f, d_model), lambda i, f: (f, 0)),
            _resident((1, d_model), lambda i, f: (0, 0)),
        ],
        out_specs=pl.BlockSpec((block_m, d_model), lambda i, f: (i, 0)),
        out_shape=jax.ShapeDtypeStruct((n_tok, d_model), F32),
        scratch_shapes=[pltpu.VMEM((block_m, d_model), BF16)],
        compiler_params=_params(("arbitrary", "arbitrary")),
        name="mlp",
    )(h, g_mlp, w_up_tiles, w_down, g_final)


def kernel(x, norm_mix_g, w_in, w_fourier, w_out, norm_mlp_g, w_up, w_down, norm_final_g):
    batch, seq, d_model = x.shape
    depth = w_in.shape[0]
    fourier_width = w_fourier.shape[1] * w_fourier.shape[2]
    n_heads = (w_in.shape[2] - fourier_width) // (3 * HEAD_DIM)
    d_ff = w_up.shape[2]
    assert w_in.shape[2] == 4 * fourier_width and n_heads * HEAD_DIM == fourier_width
    assert seq % (CLASSES * K_WINDOW) == 0
    assert depth == 1, "the MLP kernel fuses the final RMSNorm, so it handles a single layer"

    h = x.reshape(batch * seq, d_model)
    heads_nat, heads_cls = _in_proj(h, norm_mix_g[0][None, :], w_in[0].astype(BF16),
                                    batch, seq, block_m=1024)
    yf = _fourier_mix(heads_cls, w_fourier[0].astype(BF16))
    ya, w_up_tiles, w_down_b, w_out_b = _attention(
        heads_nat, heads_cls, n_heads,
        weights=((w_up[0], MLP_BLOCK_F), (w_down[0], d_model), (w_out[0], d_model)))
    h = _out_proj(yf.reshape(batch * seq, fourier_width),
                  ya.reshape(batch * seq, n_heads * HEAD_DIM),
                  h, w_out_b.reshape(d_model, d_model), block_m=512)
    h = _mlp(h, norm_mlp_g[0][None, :], w_up_tiles, w_down_b.reshape(d_ff, d_model),
             norm_final_g[None, :], block_m=1024, row_chunk=512)
    return h.reshape(batch, seq, d_model)
```
